```python
import math
import jax, jax.numpy as jnp
from jax import lax
import numpy as np

D_MODEL = 1024
BATCH = 8
SEQ = 4096
DEPTH = 2

N_MIXERS = 2
N_POOL_LAYERS = (DEPTH + 1) // 2
N_ATTN_LAYERS = DEPTH // 2
POOL_WINDOWS = (2, 4, 8, 16)
N_POOL_GROUPS = len(POOL_WINDOWS)
POOL_GROUP_DIM = D_MODEL // N_POOL_GROUPS
HEAD_DIM = 64
N_Q_HEADS = D_MODEL // HEAD_DIM
N_KV_HEADS = 4
GQA_GROUP = N_Q_HEADS // N_KV_HEADS
WINDOW = 128
BLOCK = 128
ROPE_THETA = 500000.0
ROT_DIM = HEAD_DIM // 4
D_FF = 2816
LN_EPS = 1e-5
DEEPNORM_ALPHA = (2 * DEPTH) ** 0.25
DEEPNORM_BETA = (8 * DEPTH) ** -0.25
NEG_INF = -1e30

kernel_name = "hybrid_pool_swa_sink_macaron_deepnorm"


def layer_norm(x, g, b):
    xf = x.astype(jnp.float32)
    mu = jnp.mean(xf, axis=-1, keepdims=True)
    xc = xf - mu
    var = jnp.mean(xc * xc, axis=-1, keepdims=True)
    y = xc * lax.rsqrt(var + LN_EPS) * g.astype(jnp.float32) + b.astype(jnp.float32)
    return y.astype(x.dtype)


def post_norm(x, sub, g, b):
    return layer_norm(DEEPNORM_ALPHA * x + sub, g, b)


def swiglu(x, w_gate, w_up, w_down):
    h = jax.nn.silu(x @ w_gate) * (x @ w_up)
    return h @ w_down


def pool_mixer(x, w, b, scale):
    B, S, D = x.shape
    xf = x.astype(jnp.float32)
    cs = jnp.pad(jnp.cumsum(xf, axis=1), ((0, 0), (1, 0), (0, 0)))
    csg = cs.reshape(B, S + 1, N_POOL_GROUPS, POOL_GROUP_DIM)
    t = jnp.arange(S)
    means = []
    for g, win in enumerate(POOL_WINDOWS):
        lo = jnp.maximum(t + 1 - win, 0)
        cnt = jnp.minimum(t + 1, win).astype(jnp.float32)
        seg = csg[:, 1:, g] - csg[:, lo, g]
        means.append(seg / cnt[None, :, None])
    pooled = jnp.stack(means, axis=2)
    u = (pooled - xf.reshape(B, S, N_POOL_GROUPS, POOL_GROUP_DIM)).astype(x.dtype)
    y = jnp.einsum('bsgc,gcd->bsgd', u, w).reshape(B, S, D) + b
    return y * scale


def rotary_tables(positions):
    inv_freq = ROPE_THETA ** (-jnp.arange(0, ROT_DIM, 2, dtype=jnp.float32) / ROT_DIM)
    ang = positions.astype(jnp.float32)[..., None] * inv_freq
    return jnp.cos(ang)[:, :, None, :], jnp.sin(ang)[:, :, None, :]


def apply_partial_rotary(t, cos, sin):
    tf = t.astype(jnp.float32)
    half = ROT_DIM // 2
    t1 = tf[..., :half]
    t2 = tf[..., half:ROT_DIM]
    rot = jnp.concatenate([t1 * cos - t2 * sin, t2 * cos + t1 * sin], axis=-1)
    return jnp.concatenate([rot, tf[..., ROT_DIM:]], axis=-1).astype(t.dtype)


def swa_sink_attention(x, cos, sin, w_qkv, b_qkv, sinks, w_o, b_o):
    B, S, D = x.shape
    nb = S // BLOCK
    qkv = x @ w_qkv + b_qkv
    q, k, v = jnp.split(qkv, [N_Q_HEADS * HEAD_DIM, (N_Q_HEADS + N_KV_HEADS) * HEAD_DIM], axis=-1)
    q = apply_partial_rotary(q.reshape(B, S, N_Q_HEADS, HEAD_DIM), cos, sin)
    k = apply_partial_rotary(k.reshape(B, S, N_KV_HEADS, HEAD_DIM), cos, sin)
    v = v.reshape(B, S, N_KV_HEADS, HEAD_DIM)
    q = q.reshape(B, nb, BLOCK, N_KV_HEADS, GQA_GROUP, HEAD_DIM)

    def band(t):
        t = t.reshape(B, nb, BLOCK, N_KV_HEADS, HEAD_DIM)
        prev = jnp.pad(t[:, :-1], ((0, 0), (1, 0), (0, 0), (0, 0), (0, 0)))
        return jnp.concatenate([prev, t], axis=2)

    kb, vb = band(k), band(v)
    s = jnp.einsum('bnqkgd,bnskd->bkgnqs', q, kb).astype(jnp.float32) * (HEAD_DIM ** -0.5)
    qi = jnp.arange(BLOCK)[:, None] + BLOCK
    kj = jnp.arange(2 * BLOCK)[None, :]
    diff = qi - kj
    in_win = (diff >= 0) & (diff < WINDOW)
    key_pos = jnp.arange(nb)[:, None, None] * BLOCK - BLOCK + kj[None]
    valid = in_win[None] & (key_pos >= 0)
    s = jnp.where(valid, s, NEG_INF)
    sink = sinks.astype(jnp.float32).reshape(N_KV_HEADS, GQA_GROUP)[None, :, :, None, None, None]
    m = jnp.maximum(jnp.max(s, axis=-1, keepdims=True), sink)
    p = jnp.exp(s - m)
    denom = jnp.sum(p, axis=-1, keepdims=True) + jnp.exp(sink - m)
    p = (p / denom).astype(x.dtype)
    o = jnp.einsum('bkgnqs,bnskd->bnqkgd', p, vb).reshape(B, S, N_Q_HEADS * HEAD_DIM)
    return o @ w_o + b_o


def setup_inputs(seed: int = 0) -> dict:
    key = jax.random.key(seed)
    ks = jax.random.split(key, 16)
    f32 = jnp.float32
    D, F = D_MODEL, D_FF
    QKV = (N_Q_HEADS + 2 * N_KV_HEADS) * HEAD_DIM
    x = jax.random.normal(ks[0], (BATCH, SEQ, D), f32)
    offs = jax.random.randint(ks[1], (BATCH, 1), 0, 1024, dtype=jnp.int32)
    positions = jnp.arange(SEQ, dtype=jnp.int32)[None, :] + offs
    ln_g = 1.0 + 0.02 * jax.random.normal(ks[2], (DEPTH, 3, D), f32)
    ln_b = 0.02 * jax.random.normal(ks[3], (DEPTH, 3, D), f32)
    ffn_w_gate = jax.random.normal(ks[4], (DEPTH, 2, D, F), f32) * D ** -0.5
    ffn_w_up = jax.random.normal(ks[5], (DEPTH, 2, D, F), f32) * D ** -0.5
    ffn_w_down = jax.random.normal(ks[6], (DEPTH, 2, F, D), f32) * (F ** -0.5 * DEEPNORM_BETA)
    pool_w = jax.random.normal(ks[7], (N_POOL_LAYERS, N_POOL_GROUPS, POOL_GROUP_DIM, POOL_GROUP_DIM), f32) * (POOL_GROUP_DIM ** -0.5 * DEEPNORM_BETA)
    pool_b = 0.01 * jax.random.normal(ks[8], (N_POOL_LAYERS, D), f32)
    pool_scale = 1.0 + 0.1 * jax.random.normal(ks[9], (N_POOL_LAYERS, D), f32)
    attn_w_qkv = jax.random.normal(ks[10], (N_ATTN_LAYERS, D, QKV), f32) * D ** -0.5
    attn_b_qkv = 0.01 * jax.random.normal(ks[11], (N_ATTN_LAYERS, QKV), f32)
    attn_sinks = 0.5 * jax.random.normal(ks[12], (N_ATTN_LAYERS, N_Q_HEADS), f32)
    attn_w_o = jax.random.normal(ks[13], (N_ATTN_LAYERS, N_Q_HEADS * HEAD_DIM, D), f32) * ((N_Q_HEADS * HEAD_DIM) ** -0.5 * DEEPNORM_BETA)
    attn_b_o = 0.01 * jax.random.normal(ks[14], (N_ATTN_LAYERS, D), f32)
    return {"x": x, "positions": positions, "ln_g": ln_g, "ln_b": ln_b,
            "ffn_w_gate": ffn_w_gate, "ffn_w_up": ffn_w_up, "ffn_w_down": ffn_w_down,
            "pool_w": pool_w, "pool_b": pool_b, "pool_scale": pool_scale,
            "attn_w_qkv": attn_w_qkv, "attn_b_qkv": attn_b_qkv, "attn_sinks": attn_sinks,
            "attn_w_o": attn_w_o, "attn_b_o": attn_b_o}


def reference(x, positions, ln_g, ln_b, ffn_w_gate, ffn_w_up, ffn_w_down,
              pool_w, pool_b, pool_scale,
              attn_w_qkv, attn_b_qkv, attn_sinks, attn_w_o, attn_b_o):
    cos, sin = rotary_tables(positions)
    for i in range(DEPTH):
        x = post_norm(x, 0.5 * swiglu(x, ffn_w_gate[i, 0], ffn_w_up[i, 0], ffn_w_down[i, 0]), ln_g[i, 0], ln_b[i, 0])
        j = i // N_MIXERS
        if i % N_MIXERS == 0:
            mix = pool_mixer(x, pool_w[j], pool_b[j], pool_scale[j])
        else:
            mix = swa_sink_attention(x, cos, sin, attn_w_qkv[j], attn_b_qkv[j], attn_sinks[j], attn_w_o[j], attn_b_o[j])
        x = post_norm(x, mix, ln_g[i, 1], ln_b[i, 1])
        x = post_norm(x, 0.5 * swiglu(x, ffn_w_gate[i, 1], ffn_w_up[i, 1], ffn_w_down[i, 1]), ln_g[i, 2], ln_b[i, 2])
    return x
```

```python
import functools

import jax
import jax.numpy as jnp
from jax import lax
from jax.experimental import pallas as pl
from jax.experimental.pallas import tpu as pltpu

D_MODEL = 1024
D_FF = 2816
DEPTH = 2
POOL_WINDOWS = (2, 4, 8, 16)
POOL_GROUP_DIM = D_MODEL // len(POOL_WINDOWS)
MAX_POOL_WINDOW = max(POOL_WINDOWS)
HEAD_DIM = 64
N_Q_HEADS = D_MODEL // HEAD_DIM
N_KV_HEADS = 4
GQA_GROUP = N_Q_HEADS // N_KV_HEADS
KV_DIM = N_KV_HEADS * HEAD_DIM
QKV_DIM = D_MODEL + 2 * KV_DIM
WINDOW = 128
BLOCK = 128
ROPE_THETA = 500000.0
ROT_DIM = HEAD_DIM // 4
ROT_HALF = ROT_DIM // 2
LN_EPS = 1e-5
DEEPNORM_ALPHA = (2 * DEPTH) ** 0.25
NEG_INF = -1e30

LANES = 128
HEADS_PER_LANE_TILE = LANES // HEAD_DIM
N_HEAD_PAIRS = N_Q_HEADS // HEADS_PER_LANE_TILE

FFN_TOKENS = 512
FFN_CHUNK = 256
POOL_TOKENS = 512
ATTN_TOKENS = 256
VMEM_LIMIT_BYTES = 56 * 1024 * 1024

F32 = jnp.float32
BF16 = jnp.bfloat16


def _layer_norm(z, g, b):
    mu = jnp.mean(z, axis=-1, keepdims=True)
    zc = z - mu
    var = jnp.mean(zc * zc, axis=-1, keepdims=True)
    return zc * lax.rsqrt(var + LN_EPS) * g + b


def _resident(shape):
    return pl.BlockSpec(shape, lambda *_: (0,) * len(shape), pipeline_mode=pl.Buffered(1))


def _ffn_ln_kernel(x_ref, wg_ref, wu_ref, wd_ref, g_ref, b_ref, o_ref, h_ref):
    x = x_ref[...]
    xb = x.astype(BF16)
    for c in range(D_FF // FFN_CHUNK):
        sl = slice(c * FFN_CHUNK, (c + 1) * FFN_CHUNK)
        gate = jnp.dot(xb, wg_ref[:, sl], preferred_element_type=F32)
        up = jnp.dot(xb, wu_ref[:, sl], preferred_element_type=F32)
        h = gate * (1.0 / (1.0 + jnp.exp(-gate))) * up
        h_ref[:, sl] = h.astype(BF16)
    y = jnp.dot(h_ref[...], wd_ref[...], preferred_element_type=F32)
    z = DEEPNORM_ALPHA * x + 0.5 * y
    o_ref[...] = _layer_norm(z, g_ref[...], b_ref[...])


def _ffn_ln(x2d, wg, wu, wd, g, b):
    n_tok = x2d.shape[0]
    row = lambda i: (i, 0)
    return pl.pallas_call(
        _ffn_ln_kernel,
        out_shape=jax.ShapeDtypeStruct((n_tok, D_MODEL), F32),
        grid=(n_tok // FFN_TOKENS,),
        in_specs=[
            pl.BlockSpec((FFN_TOKENS, D_MODEL), row),
            _resident((D_MODEL, D_FF)),
            _resident((D_MODEL, D_FF)),
            _resident((D_FF, D_MODEL)),
            _resident((1, D_MODEL)),
            _resident((1, D_MODEL)),
        ],
        out_specs=pl.BlockSpec((FFN_TOKENS, D_MODEL), row),
        scratch_shapes=[pltpu.VMEM((FFN_TOKENS, D_FF), BF16)],
        compiler_params=pltpu.CompilerParams(
            dimension_semantics=("arbitrary",), vmem_limit_bytes=VMEM_LIMIT_BYTES),
        name="ffn_ln",
    )(x2d, wg.astype(BF16), wu.astype(BF16), wd.astype(BF16), g.reshape(1, -1), b.reshape(1, -1))


def _pool_ln_kernel(x_ref, halo_ref, w_ref, pb_ref, ps_ref, g_ref, b_ref, o_ref):
    t = pl.program_id(1)
    x = x_ref[0]
    halo = jnp.where(t > 0, halo_ref[0], 0.0)
    ext = jnp.concatenate([halo, x], axis=0)
    pos = t * POOL_TOKENS + lax.broadcasted_iota(jnp.int32, (POOL_TOKENS, 1), 0)
    ys = []
    for k, win in enumerate(POOL_WINDOWS):
        cols = slice(k * POOL_GROUP_DIM, (k + 1) * POOL_GROUP_DIM)
        s = ext[:, cols]
        shift = 1
        while shift < win:
            s = s + pltpu.roll(s, shift, 0)
            shift *= 2
        cnt = jnp.minimum(pos + 1, win).astype(F32)
        u = s[MAX_POOL_WINDOW:] / cnt - x[:, cols]
        ys.append(jnp.dot(u.astype(BF16), w_ref[k], preferred_element_type=F32))
    y = (jnp.concatenate(ys, axis=1) + pb_ref[...]) * ps_ref[...]
    z = DEEPNORM_ALPHA * x + y
    o_ref[0] = _layer_norm(z, g_ref[...], b_ref[...])


def _pool_ln(x, w, pb, ps, g, b):
    bsz, seq, _ = x.shape
    halo_blocks = POOL_TOKENS // MAX_POOL_WINDOW
    return pl.pallas_call(
        _pool_ln_kernel,
        out_shape=jax.ShapeDtypeStruct(x.shape, F32),
        grid=(bsz, seq // POOL_TOKENS),
        in_specs=[
            pl.BlockSpec((1, POOL_TOKENS, D_MODEL), lambda i, t: (i, t, 0)),
            pl.BlockSpec((1, MAX_POOL_WINDOW, D_MODEL),
                         lambda i, t: (i, jnp.maximum(t * halo_blocks - 1, 0), 0)),
            _resident((len(POOL_WINDOWS), POOL_GROUP_DIM, POOL_GROUP_DIM)),
            _resident((1, D_MODEL)),
            _resident((1, D_MODEL)),
            _resident((1, D_MODEL)),
            _resident((1, D_MODEL)),
        ],
        out_specs=pl.BlockSpec((1, POOL_TOKENS, D_MODEL), lambda i, t: (i, t, 0)),
        compiler_params=pltpu.CompilerParams(
            dimension_semantics=("arbitrary", "arbitrary"), vmem_limit_bytes=VMEM_LIMIT_BYTES),
        name="pool_ln",
    )(x, x, w.astype(BF16), pb.reshape(1, -1), ps.reshape(1, -1), g.reshape(1, -1), b.reshape(1, -1))


def _attn_ln_kernel(x_ref, pos_ref, freq_ref, wqkv_ref, bqkv_ref, sink_ref, wo_ref, bo_ref, g_ref, b_ref,
                    o_ref, k_buf, v_buf, q_buf, att_buf):
    t = pl.program_id(1)

    @pl.when(t == 0)
    def _():
        k_buf[:, :, :BLOCK, :] = jnp.zeros((N_KV_HEADS, 2, BLOCK, LANES), BF16)
        v_buf[:, :, :BLOCK, :] = jnp.zeros((N_KV_HEADS, 2, BLOCK, LANES), BF16)

    x = x_ref[0]
    qkv = jnp.dot(x.astype(BF16), wqkv_ref[...], preferred_element_type=F32) + bqkv_ref[...]

    lane = lax.broadcasted_iota(jnp.int32, (1, LANES), 1)
    head_lane = lane % HEAD_DIM
    ang = pos_ref[0].astype(F32) * freq_ref[...]
    cos_t = jnp.where(head_lane < ROT_DIM, jnp.cos(ang), 1.0)
    sin_t = jnp.sin(ang)
    sin_lo = jnp.where(head_lane < ROT_HALF, -sin_t, 0.0)
    sin_hi = jnp.where((head_lane >= ROT_HALF) & (head_lane < ROT_DIM), sin_t, 0.0)

    def rotary(tile):
        return (tile * cos_t + pltpu.roll(tile, LANES - ROT_HALF, 1) * sin_lo
                + pltpu.roll(tile, ROT_HALF, 1) * sin_hi)

    scale = HEAD_DIM ** -0.5
    for c in range(N_HEAD_PAIRS):
        q_buf[:, c * LANES:(c + 1) * LANES] = (rotary(qkv[:, c * LANES:(c + 1) * LANES]) * scale).astype(BF16)

    low_half = lane < HEAD_DIM
    rows = slice(BLOCK, BLOCK + ATTN_TOKENS)
    for kh in range(N_KV_HEADS):
        c = kh // HEADS_PER_LANE_TILE
        own_low = kh % HEADS_PER_LANE_TILE == 0
        keep = low_half if own_low else jnp.logical_not(low_half)
        k_tile = rotary(qkv[:, D_MODEL + c * LANES:D_MODEL + (c + 1) * LANES])
        v_tile = qkv[:, D_MODEL + KV_DIM + c * LANES:D_MODEL + KV_DIM + (c + 1) * LANES]
        for buf, tile in ((k_buf, k_tile), (v_buf, v_tile)):
            own = jnp.where(keep, tile, 0.0)
            other = pltpu.roll(own, HEAD_DIM, 1)
            buf[kh, 0 if own_low else 1, rows, :] = own.astype(BF16)
            buf[kh, 1 if own_low else 0, rows, :] = other.astype(BF16)

    qi = lax.broadcasted_iota(jnp.int32, (BLOCK, 2 * BLOCK), 0) + BLOCK
    kj = lax.broadcasted_iota(jnp.int32, (BLOCK, 2 * BLOCK), 1)
    in_win = (qi - kj >= 0) & (qi - kj < WINDOW)
    nt = (((1,), (1,)), ((), ()))
    for j in range(ATTN_TOKENS // BLOCK):
        qrows = slice(j * BLOCK, (j + 1) * BLOCK)
        band = slice(j * BLOCK, (j + 2) * BLOCK)
        valid = in_win if j > 0 else in_win & ((kj >= BLOCK) | (t > 0))
        valid2 = jnp.concatenate([valid, valid], axis=0)
        for kh in range(N_KV_HEADS):
            pairs = (2 * kh, 2 * kh + 1)
            qq = jnp.concatenate([q_buf[qrows, p * LANES:(p + 1) * LANES] for p in pairs], axis=0)
            acc = None
            for half in range(HEADS_PER_LANE_TILE):
                s = lax.dot_general(qq, k_buf[kh, half, band, :], nt, preferred_element_type=F32)
                s = jnp.where(valid2, s, NEG_INF)
                sink = jnp.concatenate(
                    [jnp.full((BLOCK, 1), sink_ref[HEADS_PER_LANE_TILE * p + half], F32) for p in pairs], axis=0)
                m = jnp.maximum(jnp.max(s, axis=-1, keepdims=True), sink)
                p_un = jnp.exp(s - m)
                denom = jnp.sum(p_un, axis=-1, keepdims=True) + jnp.exp(sink - m)
                o_half = jnp.dot(p_un.astype(BF16), v_buf[kh, half, band, :], preferred_element_type=F32)
                o_half = o_half * (1.0 / denom)
                acc = o_half if acc is None else acc + o_half
            for i, p in enumerate(pairs):
                att_buf[qrows, p * LANES:(p + 1) * LANES] = acc[i * BLOCK:(i + 1) * BLOCK].astype(BF16)

    k_buf[:, :, :BLOCK, :] = k_buf[:, :, ATTN_TOKENS:ATTN_TOKENS + BLOCK, :]
    v_buf[:, :, :BLOCK, :] = v_buf[:, :, ATTN_TOKENS:ATTN_TOKENS + BLOCK, :]

    mix = jnp.dot(att_buf[...], wo_ref[...], preferred_element_type=F32) + bo_ref[...]
    z = DEEPNORM_ALPHA * x + mix
    o_ref[0] = _layer_norm(z, g_ref[...], b_ref[...])


def _attn_ln(x, positions, wqkv, bqkv, sinks, wo, bo, g, b):
    bsz, seq, _ = x.shape
    inv_freq = ROPE_THETA ** (-jnp.arange(0, ROT_DIM, 2, dtype=F32) / ROT_DIM)
    head_lane = jnp.arange(LANES) % HEAD_DIM
    freq_row = jnp.where(head_lane < ROT_DIM, inv_freq[head_lane % ROT_HALF], 0.0).reshape(1, LANES)
    return pl.pallas_call(
        _attn_ln_kernel,
        out_shape=jax.ShapeDtypeStruct(x.shape, F32),
        grid=(bsz, seq // ATTN_TOKENS),
        in_specs=[
            pl.BlockSpec((1, ATTN_TOKENS, D_MODEL), lambda i, t: (i, t, 0)),
            pl.BlockSpec((1, ATTN_TOKENS, 1), lambda i, t: (i, t, 0)),
            _resident((1, LANES)),
            _resident((D_MODEL, QKV_DIM)),
            _resident((1, QKV_DIM)),
            pl.BlockSpec(memory_space=pltpu.SMEM),
            _resident((D_MODEL, D_MODEL)),
            _resident((1, D_MODEL)),
            _resident((1, D_MODEL)),
            _resident((1, D_MODEL)),
        ],
        out_specs=pl.BlockSpec((1, ATTN_TOKENS, D_MODEL), lambda i, t: (i, t, 0)),
        scratch_shapes=[
            pltpu.VMEM((N_KV_HEADS, 2, BLOCK + ATTN_TOKENS, LANES), BF16),
            pltpu.VMEM((N_KV_HEADS, 2, BLOCK + ATTN_TOKENS, LANES), BF16),
            pltpu.VMEM((ATTN_TOKENS, D_MODEL), BF16),
            pltpu.VMEM((ATTN_TOKENS, D_MODEL), BF16),
        ],
        compiler_params=pltpu.CompilerParams(
            dimension_semantics=("arbitrary", "arbitrary"), vmem_limit_bytes=VMEM_LIMIT_BYTES),
        name="attn_ln",
    )(x, positions.reshape(bsz, seq, 1), freq_row, wqkv.astype(BF16), bqkv.reshape(1, -1), sinks,
      wo.astype(BF16), bo.reshape(1, -1), g.reshape(1, -1), b.reshape(1, -1))


def kernel(x, positions, ln_g, ln_b, ffn_w_gate, ffn_w_up, ffn_w_down, pool_w, pool_b, pool_scale,
           attn_w_qkv, attn_b_qkv, attn_sinks, attn_w_o, attn_b_o):
    bsz, seq, d = x.shape

    def ffn(h, i, s, n):
        out = _ffn_ln(h.reshape(bsz * seq, d), ffn_w_gate[i, s], ffn_w_up[i, s], ffn_w_down[i, s],
                      ln_g[i, n], ln_b[i, n])
        return out.reshape(bsz, seq, d)

    for i in range(DEPTH):
        x = ffn(x, i, 0, 0)
        j = i // 2
        if i % 2 == 0:
            x = _pool_ln(x, pool_w[j], pool_b[j], pool_scale[j], ln_g[i, 1], ln_b[i, 1])
        else:
            x = _attn_ln(x, positions, attn_w_qkv[j], attn_b_qkv[j], attn_sinks[j], attn_w_o[j], attn_b_o[j],
                         ln_g[i, 1], ln_b[i, 1])
        x = ffn(x, i, 1, 2)
    return x
```

```python
import math

import jax
import jax.numpy as jnp
import numpy as np
from jax import lax
from jax.experimental import pallas as pl
from jax.experimental.pallas import tpu as pltpu

D_MODEL = 1024
D_FF = 2816
DEPTH = 2
POOL_WINDOWS = (2, 4, 8, 16)
POOL_GROUP_DIM = D_MODEL // len(POOL_WINDOWS)
MAX_POOL_WINDOW = max(POOL_WINDOWS)
HEAD_DIM = 64
N_Q_HEADS = D_MODEL // HEAD_DIM
N_KV_HEADS = 4
KV_DIM = N_KV_HEADS * HEAD_DIM
QKV_DIM = D_MODEL + 2 * KV_DIM
WINDOW = 128
BLOCK = 128
ROPE_THETA = 500000.0
ROT_DIM = HEAD_DIM // 4
ROT_HALF = ROT_DIM // 2
LN_EPS = 1e-5
DEEPNORM_ALPHA = (2 * DEPTH) ** 0.25
NEG_INF = -1e30
LOG2_E = math.log2(math.e)

LANES = 128
LANE_GROUP = 8
N_HEAD_TILES = N_Q_HEADS // 2
TILES_PER_KV_HEAD = N_HEAD_TILES // N_KV_HEADS

FFN_TOKENS = 1024
FFN_SUBTILE = 256
FFN_CHUNK = 256
POOL_TOKENS = 512
ATTN_TOKENS = 512
ATTN_OUT_ROWS = 256
VMEM_LIMIT_BYTES = 56 * 1024 * 1024

F32 = jnp.float32
BF16 = jnp.bfloat16


def _layer_norm(z, g, b):
    mu = jnp.mean(z, axis=-1, keepdims=True)
    zc = z - mu
    var = jnp.mean(zc * zc, axis=-1, keepdims=True)
    return zc * lax.rsqrt(var + LN_EPS) * g + b


def _resident(shape):
    return pl.BlockSpec(shape, lambda *_: (0,) * len(shape), pipeline_mode=pl.Buffered(1))


def _ffn_ln_kernel(x_ref, wg_ref, wu_ref, wd_ref, g_ref, b_ref, o_ref, h_ref):
    for r in range(FFN_TOKENS // FFN_SUBTILE):
        rows = slice(r * FFN_SUBTILE, (r + 1) * FFN_SUBTILE)
        x = x_ref[rows, :]
        xb = x.astype(BF16)
        for c in range(D_FF // FFN_CHUNK):
            sl = slice(c * FFN_CHUNK, (c + 1) * FFN_CHUNK)
            gate = jnp.dot(xb, wg_ref[:, sl], preferred_element_type=F32)
            up = jnp.dot(xb, wu_ref[:, sl], preferred_element_type=F32)
            h = gate * (1.0 / (1.0 + jnp.exp(-gate))) * up
            h_ref[rows, sl] = h.astype(BF16)
        y = jnp.dot(h_ref[rows, :], wd_ref[...], preferred_element_type=F32)
        z = DEEPNORM_ALPHA * x + 0.5 * y
        o_ref[rows, :] = _layer_norm(z, g_ref[...], b_ref[...])


def _ffn_ln(x2d, wg, wu, wd, g, b):
    n_tok = x2d.shape[0]
    row = lambda i: (i, 0)
    return pl.pallas_call(
        _ffn_ln_kernel,
        out_shape=jax.ShapeDtypeStruct((n_tok, D_MODEL), F32),
        grid=(n_tok // FFN_TOKENS,),
        in_specs=[
            pl.BlockSpec((FFN_TOKENS, D_MODEL), row),
            _resident((D_MODEL, D_FF)),
            _resident((D_MODEL, D_FF)),
            _resident((D_FF, D_MODEL)),
            _resident((1, D_MODEL)),
            _resident((1, D_MODEL)),
        ],
        out_specs=pl.BlockSpec((FFN_TOKENS, D_MODEL), row),
        scratch_shapes=[pltpu.VMEM((FFN_TOKENS, D_FF), BF16)],
        compiler_params=pltpu.CompilerParams(
            dimension_semantics=("arbitrary",), vmem_limit_bytes=VMEM_LIMIT_BYTES),
        name="ffn_ln",
    )(x2d, wg.astype(BF16), wu.astype(BF16), wd.astype(BF16), g.reshape(1, -1), b.reshape(1, -1))


def _pool_ln_kernel(x_ref, halo_ref, w_ref, pb_ref, ps_ref, g_ref, b_ref, o_ref):
    t = pl.program_id(1)
    x = x_ref[0]
    halo = jnp.where(t > 0, halo_ref[0], 0.0)
    ext = jnp.concatenate([halo, x], axis=0)
    pos = t * POOL_TOKENS + lax.broadcasted_iota(jnp.int32, (POOL_TOKENS, 1), 0)
    ys = []
    for k, win in enumerate(POOL_WINDOWS):
        cols = slice(k * POOL_GROUP_DIM, (k + 1) * POOL_GROUP_DIM)
        s = ext[:, cols]
        shift = 1
        while shift < win:
            s = s + pltpu.roll(s, shift, 0)
            shift *= 2
        cnt = jnp.minimum(pos + 1, win).astype(F32)
        u = s[MAX_POOL_WINDOW:] / cnt - x[:, cols]
        ys.append(jnp.dot(u.astype(BF16), w_ref[k], preferred_element_type=F32))
    y = (jnp.concatenate(ys, axis=1) + pb_ref[...]) * ps_ref[...]
    z = DEEPNORM_ALPHA * x + y
    o_ref[0] = _layer_norm(z, g_ref[...], b_ref[...])


def _pool_ln(x, w, pb, ps, g, b):
    bsz, seq, _ = x.shape
    halo_blocks = POOL_TOKENS // MAX_POOL_WINDOW
    return pl.pallas_call(
        _pool_ln_kernel,
        out_shape=jax.ShapeDtypeStruct(x.shape, F32),
        grid=(bsz, seq // POOL_TOKENS),
        in_specs=[
            pl.BlockSpec((1, POOL_TOKENS, D_MODEL), lambda i, t: (i, t, 0)),
            pl.BlockSpec((1, MAX_POOL_WINDOW, D_MODEL),
                         lambda i, t: (i, jnp.maximum(t * halo_blocks - 1, 0), 0)),
            _resident((len(POOL_WINDOWS), POOL_GROUP_DIM, POOL_GROUP_DIM)),
            _resident((1, D_MODEL)),
            _resident((1, D_MODEL)),
            _resident((1, D_MODEL)),
            _resident((1, D_MODEL)),
        ],
        out_specs=pl.BlockSpec((1, POOL_TOKENS, D_MODEL), lambda i, t: (i, t, 0)),
        compiler_params=pltpu.CompilerParams(
            dimension_semantics=("arbitrary", "arbitrary"), vmem_limit_bytes=VMEM_LIMIT_BYTES),
        name="pool_ln",
    )(x, x, w.astype(BF16), pb.reshape(1, -1), ps.reshape(1, -1), g.reshape(1, -1), b.reshape(1, -1))


def _interleave_heads(w):
    lead = w.shape[:-1]
    n_tiles = w.shape[-1] // LANES
    w = w.reshape(*lead, n_tiles, 2, HEAD_DIM // LANE_GROUP, LANE_GROUP)
    a = w.ndim - 2
    grp = lambda lo, hi: lax.slice_in_dim(w, lo, hi, axis=a)
    w = jnp.concatenate([grp(0, 1), grp(2, 5), grp(1, 2), grp(5, 8)], axis=a)
    w = jnp.swapaxes(w, a - 1, a)
    return w.reshape(*lead, n_tiles * LANES)


def _rope_expand_matrix():
    e = np.zeros((2 * ROT_HALF, 2 * LANES), np.float32)
    for lane in range(LANES):
        group, r = divmod(lane, LANE_GROUP)
        if group % (LANES // LANE_GROUP // 2) < 2:
            e[r, lane] = 1.0
            e[ROT_HALF + r, LANES + lane] = -1.0 if group < 2 else 1.0
    return np.concatenate([e, e, e], axis=0)


def _band_bias():
    qi = np.arange(BLOCK)[:, None] + BLOCK
    kj = np.arange(2 * BLOCK)[None, :]
    in_win = (qi - kj >= 0) & (qi - kj < WINDOW)
    first = in_win & (kj >= BLOCK)
    return np.where(np.stack([in_win, first]), 0.0, NEG_INF).astype(np.float32)


def _attn_ln_kernel(x_ref, pos_ref, freq_ref, rope_ref, bias_ref, wqkv_ref, bqkv_ref, sink_ref, wo_ref, bo_ref,
                    g_ref, b_ref, o_ref, q_buf, k_buf, v_buf, att_buf, p_buf, e_buf):
    t = pl.program_id(1)

    @pl.when(t == 0)
    def _():
        k_buf[:, :, :BLOCK, :] = jnp.zeros((N_KV_HEADS, 2, BLOCK, LANES), BF16)
        v_buf[:, :, :BLOCK, :] = jnp.zeros((N_KV_HEADS, 2, BLOCK, LANES), BF16)

    qkv = jnp.dot(x_ref[0].astype(BF16), wqkv_ref[...], preferred_element_type=F32) + bqkv_ref[...]

    def qkv_tile(col):
        return qkv[:, col:col + LANES]

    ang = freq_ref[...] * pos_ref[0, 0].astype(F32)
    trig = jnp.concatenate([jnp.cos(ang), jnp.sin(ang)], axis=0)
    hi = trig.astype(BF16)
    rest = trig - hi.astype(F32)
    mid = rest.astype(BF16)
    lo = (rest - mid.astype(F32)).astype(BF16)
    tables = lax.dot_general(jnp.concatenate([hi, mid, lo], axis=0), rope_ref[...], (((0,), (0,)), ((), ())),
                             preferred_element_type=F32)
    lane_group = lax.broadcasted_iota(jnp.int32, (1, LANES), 1) // LANE_GROUP
    lanes_a = lane_group % 2 == 0
    is_rot = lane_group % (LANES // LANE_GROUP // 2) < 2
    cos_t = tables[:, :LANES] + jnp.where(is_rot, 0.0, 1.0)
    sin_signed = tables[:, LANES:]

    def rotary(tile):
        return tile * cos_t + pltpu.roll(tile, LANES // 2, 1) * sin_signed

    scale = HEAD_DIM ** -0.5 * LOG2_E
    for p in range(N_HEAD_TILES):
        q_buf[p] = (rotary(qkv_tile(p * LANES)) * scale).astype(BF16)

    rows = slice(BLOCK, BLOCK + ATTN_TOKENS)
    for kh in range(N_KV_HEADS):
        c, par = kh // 2, kh % 2
        own = lanes_a if par == 0 else jnp.logical_not(lanes_a)
        to_other = LANE_GROUP if par == 0 else LANES - LANE_GROUP
        k_tile = rotary(qkv_tile(D_MODEL + c * LANES))
        v_tile = qkv_tile(D_MODEL + KV_DIM + c * LANES)
        k_own = jnp.where(own, k_tile, 0.0)
        v_moved = pltpu.roll(jnp.where(own, v_tile, 0.0), to_other, 1)
        k_buf[kh, par, rows, :] = k_own.astype(BF16)
        k_buf[kh, 1 - par, rows, :] = pltpu.roll(k_own, to_other, 1).astype(BF16)
        v_buf[kh, par, rows, :] = jnp.where(own, v_tile, 1.0).astype(BF16)
        v_buf[kh, 1 - par, rows, :] = jnp.where(own, 1.0, v_moved).astype(BF16)

    nt = (((1,), (1,)), ((), ()))
    n_blocks = ATTN_TOKENS // BLOCK

    def scores_stage(j, slot):
        r0 = pl.multiple_of(j * BLOCK, BLOCK)
        qrows = pl.ds(r0, BLOCK)
        band = pl.ds(r0, 2 * BLOCK)
        bias = bias_ref[jnp.where(t * n_blocks + j == 0, 1, 0)]
        for kh in range(N_KV_HEADS):
            tiles = [TILES_PER_KV_HEAD * kh + i for i in range(TILES_PER_KV_HEAD)]
            qq = jnp.concatenate([q_buf[p, qrows, :] for p in tiles], axis=0)
            m_all = []
            for par in range(2):
                s = lax.dot_general(qq, k_buf[kh, par, band, :], nt, preferred_element_type=F32)
                ms = []
                for i, p in enumerate(tiles):
                    sink = sink_ref[2 * p + par] * LOG2_E
                    sh = s[i * BLOCK:(i + 1) * BLOCK] + bias
                    m = jnp.maximum(jnp.max(sh, axis=-1, keepdims=True), sink)
                    p_buf[slot, kh, par, i * BLOCK:(i + 1) * BLOCK, :] = jnp.exp2(sh - m).astype(BF16)
                    ms.append(sink - jnp.broadcast_to(m, (BLOCK, LANES)))
                m_all.append(jnp.concatenate(ms, axis=0))
            e_buf[slot, kh] = jnp.exp2(jnp.where(lanes_a, m_all[0], m_all[1]))

    def values_stage(j, slot):
        r0 = pl.multiple_of(j * BLOCK, BLOCK)
        qrows = pl.ds(r0, BLOCK)
        band = pl.ds(r0, 2 * BLOCK)
        for kh in range(N_KV_HEADS):
            o_full = [jnp.dot(p_buf[slot, kh, par], v_buf[kh, par, band, :], preferred_element_type=F32)
                      for par in range(2)]
            num = jnp.where(lanes_a, o_full[0], o_full[1])
            den = pltpu.roll(jnp.where(lanes_a, o_full[1], o_full[0]), LANE_GROUP, 1) + e_buf[slot, kh]
            out = (num / den).astype(BF16)
            for i in range(TILES_PER_KV_HEAD):
                att_buf[TILES_PER_KV_HEAD * kh + i, qrows, :] = out[i * BLOCK:(i + 1) * BLOCK]

    def block_body(j, carry):
        values_stage(j, j % 2)
        scores_stage(j + 1, (j + 1) % 2)
        return carry

    scores_stage(0, 0)
    lax.fori_loop(0, n_blocks - 1, block_body, 0)
    values_stage(n_blocks - 1, (n_blocks - 1) % 2)

    k_buf[:, :, :BLOCK, :] = k_buf[:, :, ATTN_TOKENS:ATTN_TOKENS + BLOCK, :]
    v_buf[:, :, :BLOCK, :] = v_buf[:, :, ATTN_TOKENS:ATTN_TOKENS + BLOCK, :]

    for r in range(ATTN_TOKENS // ATTN_OUT_ROWS):
        rs = slice(r * ATTN_OUT_ROWS, (r + 1) * ATTN_OUT_ROWS)
        att = jnp.concatenate([att_buf[p, rs, :] for p in range(N_HEAD_TILES)], axis=1)
        mix = jnp.dot(att, wo_ref[...], preferred_element_type=F32) + bo_ref[...]
        z = DEEPNORM_ALPHA * x_ref[0, rs, :] + mix
        o_ref[0, rs, :] = _layer_norm(z, g_ref[...], b_ref[...])


def _attn_ln(x, positions, wqkv, bqkv, sinks, wo, bo, g, b):
    bsz, seq, _ = x.shape
    n_tiles = seq // ATTN_TOKENS
    inv_freq = ROPE_THETA ** (-jnp.arange(0, ROT_DIM, 2, dtype=F32) / ROT_DIM)
    split = (D_MODEL, D_MODEL + KV_DIM)
    wqkv_l = jnp.concatenate([_interleave_heads(w) for w in jnp.split(wqkv, split, axis=1)], axis=1)
    bqkv_l = jnp.concatenate([_interleave_heads(w) for w in jnp.split(bqkv, split)])
    wo_l = _interleave_heads(wo.T).T
    return pl.pallas_call(
        _attn_ln_kernel,
        out_shape=jax.ShapeDtypeStruct(x.shape, F32),
        grid=(bsz, n_tiles),
        in_specs=[
            pl.BlockSpec((1, ATTN_TOKENS, D_MODEL), lambda i, t: (i, t, 0)),
            pl.BlockSpec((1, 1, 1, ATTN_TOKENS), lambda i, t: (i, t, 0, 0)),
            _resident((ROT_HALF, 1)),
            _resident((6 * ROT_HALF, 2 * LANES)),
            _resident((2, BLOCK, 2 * BLOCK)),
            _resident((D_MODEL, QKV_DIM)),
            _resident((1, QKV_DIM)),
            pl.BlockSpec(memory_space=pltpu.SMEM),
            _resident((D_MODEL, D_MODEL)),
            _resident((1, D_MODEL)),
            _resident((1, D_MODEL)),
            _resident((1, D_MODEL)),
        ],
        out_specs=pl.BlockSpec((1, ATTN_TOKENS, D_MODEL), lambda i, t: (i, t, 0)),
        scratch_shapes=[
            pltpu.VMEM((N_HEAD_TILES, ATTN_TOKENS, LANES), BF16),
            pltpu.VMEM((N_KV_HEADS, 2, BLOCK + ATTN_TOKENS, LANES), BF16),
            pltpu.VMEM((N_KV_HEADS, 2, BLOCK + ATTN_TOKENS, LANES), BF16),
            pltpu.VMEM((N_HEAD_TILES, ATTN_TOKENS, LANES), BF16),
            pltpu.VMEM((2, N_KV_HEADS, 2, TILES_PER_KV_HEAD * BLOCK, 2 * BLOCK), BF16),
            pltpu.VMEM((2, N_KV_HEADS, TILES_PER_KV_HEAD * BLOCK, LANES), F32),
        ],
        compiler_params=pltpu.CompilerParams(
            dimension_semantics=("arbitrary", "arbitrary"), vmem_limit_bytes=VMEM_LIMIT_BYTES),
        name="attn_ln",
    )(x, positions.reshape(bsz, n_tiles, 1, ATTN_TOKENS), inv_freq.reshape(ROT_HALF, 1),
      jnp.asarray(_rope_expand_matrix(), BF16), jnp.asarray(_band_bias()), wqkv_l.astype(BF16),
      bqkv_l.reshape(1, -1), sinks, wo_l.astype(BF16), bo.reshape(1, -1), g.reshape(1, -1), b.reshape(1, -1))


def kernel(x, positions, ln_g, ln_b, ffn_w_gate, ffn_w_up, ffn_w_down, pool_w, pool_b, pool_scale,
           attn_w_qkv, attn_b_qkv, attn_sinks, attn_w_o, attn_b_o):
    bsz, seq, d = x.shape

    def ffn(h, i, s, n):
        out = _ffn_ln(h.reshape(bsz * seq, d), ffn_w_gate[i, s], ffn_w_up[i, s], ffn_w_down[i, s],
                      ln_g[i, n], ln_b[i, n])
        return out.reshape(bsz, seq, d)

    for i in range(DEPTH):
        x = ffn(x, i, 0, 0)
        j = i // 2
        if i % 2 == 0:
            x = _pool_ln(x, pool_w[j], pool_b[j], pool_scale[j], ln_g[i, 1], ln_b[i, 1])
        else:
            x = _attn_ln(x, positions, attn_w_qkv[j], attn_b_qkv[j], attn_sinks[j], attn_w_o[j], attn_b_o[j],
                         ln_g[i, 1], ln_b[i, 1])
        x = ffn(x, i, 1, 2)
    return x
```

```python
import functools
import math

import jax
import jax.numpy as jnp
import numpy as np
from jax import lax
from jax.experimental import pallas as pl
from jax.experimental.pallas import tpu as pltpu

D_MODEL = 1024
D_FF = 2816
DEPTH = 2
POOL_WINDOWS = (2, 4, 8, 16)
POOL_GROUP_DIM = D_MODEL // len(POOL_WINDOWS)
MAX_POOL_WINDOW = max(POOL_WINDOWS)
HEAD_DIM = 64
N_Q_HEADS = D_MODEL // HEAD_DIM
N_KV_HEADS = 4
KV_DIM = N_KV_HEADS * HEAD_DIM
QKV_DIM = D_MODEL + 2 * KV_DIM
WINDOW = 128
BLOCK = 128
ROPE_THETA = 500000.0
ROT_DIM = HEAD_DIM // 4
ROT_HALF = ROT_DIM // 2
LN_EPS = 1e-5
DEEPNORM_ALPHA = (2 * DEPTH) ** 0.25
NEG_INF = -1e30
LOG2_E = math.log2(math.e)

LANES = 128
LANE_GROUP = 8
N_HEAD_TILES = N_Q_HEADS // 2
TILES_PER_KV_HEAD = N_HEAD_TILES // N_KV_HEADS

FFN_TOKENS = 1024
FFN_SUBTILE = 256
FFN_CHUNK = 256
ATTN_TOKENS = 512
ATTN_OUT_ROWS = 256
VMEM_LIMIT_BYTES = 56 * 1024 * 1024

F32 = jnp.float32
BF16 = jnp.bfloat16


def _layer_norm(z, g, b):
    mu = jnp.mean(z, axis=-1, keepdims=True)
    zc = z - mu
    var = jnp.mean(zc * zc, axis=-1, keepdims=True)
    return zc * lax.rsqrt(var + LN_EPS) * g + b


def _resident(shape):
    return pl.BlockSpec(shape, lambda *_: (0,) * len(shape), pipeline_mode=pl.Buffered(1))


def _ffn_ln_body(src_ref, wg_ref, wu_ref, wd_ref, g_ref, b_ref, o_ref, h_ref, next_subtile_tasks=None):
    n_sub = FFN_TOKENS // FFN_SUBTILE
    for r in range(n_sub):
        rows = slice(r * FFN_SUBTILE, (r + 1) * FFN_SUBTILE)
        pending = next_subtile_tasks(r + 1) if next_subtile_tasks and r + 1 < n_sub else []
        x = src_ref[rows, :]
        xb = x.astype(BF16)
        for c in range(D_FF // FFN_CHUNK):
            sl = slice(c * FFN_CHUNK, (c + 1) * FFN_CHUNK)
            gate = jnp.dot(xb, wg_ref[:, sl], preferred_element_type=F32)
            up = jnp.dot(xb, wu_ref[:, sl], preferred_element_type=F32)
            h = gate * (1.0 / (1.0 + jnp.exp(-gate))) * up
            h_ref[rows, sl] = h.astype(BF16)
            if pending:
                pending.pop(0)()
        for task in pending:
            task()
        y = jnp.dot(h_ref[rows, :], wd_ref[...], preferred_element_type=F32)
        z = DEEPNORM_ALPHA * x + 0.5 * y
        o_ref[rows, :] = _layer_norm(z, g_ref[...], b_ref[...])


def _ffn_ln_kernel(x_ref, wg_ref, wu_ref, wd_ref, g_ref, b_ref, o_ref, h_ref):
    _ffn_ln_body(x_ref, wg_ref, wu_ref, wd_ref, g_ref, b_ref, o_ref, h_ref)


def _pool_mixer_tasks(x_ref, halo_ref, r, seq_row0, pw_ref, pb_ref, ps_ref, g_ref, b_ref, mix_ref):
    rows = slice(r * FFN_SUBTILE, (r + 1) * FFN_SUBTILE)
    st = {}

    def load():
        st["x"] = x_ref[rows, :]
        if r == 0:
            st["prev"] = jnp.where(seq_row0 > 0, halo_ref[...], 0.0)
        else:
            st["prev"] = x_ref[r * FFN_SUBTILE - MAX_POOL_WINDOW:r * FFN_SUBTILE, :]
        st["pos"] = seq_row0 + r * FFN_SUBTILE + lax.broadcasted_iota(jnp.int32, (FFN_SUBTILE, 1), 0)
        st["ys"] = []

    def group(k):
        def run():
            win = POOL_WINDOWS[k]
            cols = slice(k * POOL_GROUP_DIM, (k + 1) * POOL_GROUP_DIM)
            s = jnp.concatenate([st["prev"][:, cols], st["x"][:, cols]], axis=0)
            shift = 1
            while shift < win:
                s = s + pltpu.roll(s, shift, 0)
                shift *= 2
            cnt = jnp.minimum(st["pos"] + 1, win).astype(F32)
            u = s[MAX_POOL_WINDOW:] / cnt - st["x"][:, cols]
            st["ys"].append(jnp.dot(u.astype(BF16), pw_ref[k], preferred_element_type=F32))
        return run

    def finish():
        y = (jnp.concatenate(st["ys"], axis=1) + pb_ref[...]) * ps_ref[...]
        z = DEEPNORM_ALPHA * st["x"] + y
        mix_ref[rows, :] = _layer_norm(z, g_ref[...], b_ref[...])

    return [load] + [group(k) for k in range(len(POOL_WINDOWS))] + [finish]


def _pool_ffn_ln_kernel(x_ref, halo_ref, pw_ref, pb_ref, ps_ref, pg_ref, pbeta_ref, wg_ref, wu_ref, wd_ref,
                        g_ref, b_ref, o_ref, h_ref, mix_ref, *, seq):
    seq_row0 = (pl.program_id(0) * FFN_TOKENS) % seq

    def tasks(r):
        return _pool_mixer_tasks(x_ref, halo_ref, r, seq_row0, pw_ref, pb_ref, ps_ref, pg_ref, pbeta_ref, mix_ref)

    for task in tasks(0):
        task()
    _ffn_ln_body(mix_ref, wg_ref, wu_ref, wd_ref, g_ref, b_ref, o_ref, h_ref, next_subtile_tasks=tasks)


def _ffn_ln(x2d, wg_all, wu_all, wd_all, layer, slot, g, b, pool=None, seq=None):
    n_tok = x2d.shape[0]
    row = lambda i: (i, 0)
    pick = lambda i: (layer, slot, 0, 0)
    vec = lambda v: v.reshape(1, -1)
    ffn_specs = [
        pl.BlockSpec((None, None, D_MODEL, D_FF), pick, pipeline_mode=pl.Buffered(1)),
        pl.BlockSpec((None, None, D_MODEL, D_FF), pick, pipeline_mode=pl.Buffered(1)),
        pl.BlockSpec((None, None, D_FF, D_MODEL), pick, pipeline_mode=pl.Buffered(1)),
        _resident((1, D_MODEL)),
        _resident((1, D_MODEL)),
    ]
    ffn_args = (wg_all, wu_all, wd_all, vec(g), vec(b))
    scratch = [pltpu.VMEM((FFN_TOKENS, D_FF), BF16)]
    if pool is None:
        body, in_specs, args = _ffn_ln_kernel, [pl.BlockSpec((FFN_TOKENS, D_MODEL), row)] + ffn_specs, (x2d,) + ffn_args
    else:
        pw, pb, ps, pg, pbeta = pool
        halo_blocks = FFN_TOKENS // MAX_POOL_WINDOW
        body = functools.partial(_pool_ffn_ln_kernel, seq=seq)
        in_specs = [
            pl.BlockSpec((FFN_TOKENS, D_MODEL), row),
            pl.BlockSpec((MAX_POOL_WINDOW, D_MODEL), lambda i: (jnp.maximum(i * halo_blocks - 1, 0), 0)),
            _resident((len(POOL_WINDOWS), POOL_GROUP_DIM, POOL_GROUP_DIM)),
            _resident((1, D_MODEL)),
            _resident((1, D_MODEL)),
            _resident((1, D_MODEL)),
            _resident((1, D_MODEL)),
        ] + ffn_specs
        args = (x2d, x2d, pw.astype(BF16), vec(pb), vec(ps), vec(pg), vec(pbeta)) + ffn_args
        scratch.append(pltpu.VMEM((FFN_TOKENS, D_MODEL), F32))
    return pl.pallas_call(
        body,
        out_shape=jax.ShapeDtypeStruct((n_tok, D_MODEL), F32),
        grid=(n_tok // FFN_TOKENS,),
        in_specs=in_specs,
        out_specs=pl.BlockSpec((FFN_TOKENS, D_MODEL), row),
        scratch_shapes=scratch,
        compiler_params=pltpu.CompilerParams(
            dimension_semantics=("arbitrary",), vmem_limit_bytes=VMEM_LIMIT_BYTES),
        name="ffn_ln" if pool is None else "pool_ffn_ln",
    )(*args)


def _interleave_heads(w):
    lead = w.shape[:-1]
    n_tiles = w.shape[-1] // LANES
    w = w.reshape(*lead, n_tiles, 2, HEAD_DIM // LANE_GROUP, LANE_GROUP)
    a = w.ndim - 2
    grp = lambda lo, hi: lax.slice_in_dim(w, lo, hi, axis=a)
    w = jnp.concatenate([grp(0, 1), grp(2, 5), grp(1, 2), grp(5, 8)], axis=a)
    w = jnp.swapaxes(w, a - 1, a)
    return w.reshape(*lead, n_tiles * LANES)


def _rope_expand_matrix():
    e = np.zeros((2 * ROT_HALF, 2 * LANES), np.float32)
    for lane in range(LANES):
        group, r = divmod(lane, LANE_GROUP)
        if group % (LANES // LANE_GROUP // 2) < 2:
            e[r, lane] = 1.0
            e[ROT_HALF + r, LANES + lane] = -1.0 if group < 2 else 1.0
    return np.concatenate([e, e, e], axis=0)


def _band_bias():
    qi = np.arange(BLOCK)[:, None] + BLOCK
    kj = np.arange(2 * BLOCK)[None, :]
    in_win = (qi - kj >= 0) & (qi - kj < WINDOW)
    first = in_win & (kj >= BLOCK)
    return np.where(np.stack([in_win, first]), 0.0, NEG_INF).astype(np.float32)


def _attn_ln_kernel(x_ref, pos_ref, freq_ref, rope_ref, bias_ref, wqkv_ref, bqkv_ref, sink_ref, wo_ref, bo_ref,
                    g_ref, b_ref, o_ref, q_buf, k_buf, v_buf, att_buf, p_buf, e_buf):
    t = pl.program_id(1)

    @pl.when(t == 0)
    def _():
        k_buf[:, :, :BLOCK, :] = jnp.zeros((N_KV_HEADS, 2, BLOCK, LANES), BF16)
        v_buf[:, :, :BLOCK, :] = jnp.zeros((N_KV_HEADS, 2, BLOCK, LANES), BF16)

    qkv = jnp.dot(x_ref[0].astype(BF16), wqkv_ref[...], preferred_element_type=F32) + bqkv_ref[...]

    def qkv_tile(col):
        return qkv[:, col:col + LANES]

    ang = freq_ref[...] * pos_ref[0, 0].astype(F32)
    trig = jnp.concatenate([jnp.cos(ang), jnp.sin(ang)], axis=0)
    hi = trig.astype(BF16)
    rest = trig - hi.astype(F32)
    mid = rest.astype(BF16)
    lo = (rest - mid.astype(F32)).astype(BF16)
    tables = lax.dot_general(jnp.concatenate([hi, mid, lo], axis=0), rope_ref[...], (((0,), (0,)), ((), ())),
                             preferred_element_type=F32)
    lane_group = lax.broadcasted_iota(jnp.int32, (1, LANES), 1) // LANE_GROUP
    lanes_a = lane_group % 2 == 0
    is_rot = lane_group % (LANES // LANE_GROUP // 2) < 2
    cos_t = tables[:, :LANES] + jnp.where(is_rot, 0.0, 1.0)
    sin_signed = tables[:, LANES:]

    def rotary(tile):
        return tile * cos_t + pltpu.roll(tile, LANES // 2, 1) * sin_signed

    scale = HEAD_DIM ** -0.5 * LOG2_E
    for p in range(N_HEAD_TILES):
        q_buf[p] = (rotary(qkv_tile(p * LANES)) * scale).astype(BF16)

    rows = slice(BLOCK, BLOCK + ATTN_TOKENS)
    for kh in range(N_KV_HEADS):
        c, par = kh // 2, kh % 2
        own = lanes_a if par == 0 else jnp.logical_not(lanes_a)
        to_other = LANE_GROUP if par == 0 else LANES - LANE_GROUP
        k_tile = rotary(qkv_tile(D_MODEL + c * LANES))
        v_tile = qkv_tile(D_MODEL + KV_DIM + c * LANES)
        k_own = jnp.where(own, k_tile, 0.0)
        v_moved = pltpu.roll(jnp.where(own, v_tile, 0.0), to_other, 1)
        k_buf[kh, par, rows, :] = k_own.astype(BF16)
        k_buf[kh, 1 - par, rows, :] = pltpu.roll(k_own, to_other, 1).astype(BF16)
        v_buf[kh, par, rows, :] = jnp.where(own, v_tile, 1.0).astype(BF16)
        v_buf[kh, 1 - par, rows, :] = jnp.where(own, 1.0, v_moved).astype(BF16)

    nt = (((1,), (1,)), ((), ()))
    n_blocks = ATTN_TOKENS // BLOCK

    def scores_stage(j, slot):
        r0 = pl.multiple_of(j * BLOCK, BLOCK)
        qrows = pl.ds(r0, BLOCK)
        band = pl.ds(r0, 2 * BLOCK)
        bias = bias_ref[jnp.where(t * n_blocks + j == 0, 1, 0)]
        for kh in range(N_KV_HEADS):
            tiles = [TILES_PER_KV_HEAD * kh + i for i in range(TILES_PER_KV_HEAD)]
            qq = jnp.concatenate([q_buf[p, qrows, :] for p in tiles], axis=0)
            m_all = []
            for par in range(2):
                s = lax.dot_general(qq, k_buf[kh, par, band, :], nt, preferred_element_type=F32)
                ms = []
                for i, p in enumerate(tiles):
                    sink = sink_ref[2 * p + par] * LOG2_E
                    sh = s[i * BLOCK:(i + 1) * BLOCK] + bias
                    m = jnp.maximum(jnp.max(sh, axis=-1, keepdims=True), sink)
                    p_buf[slot, kh, par, i * BLOCK:(i + 1) * BLOCK, :] = jnp.exp2(sh - m).astype(BF16)
                    ms.append(sink - jnp.broadcast_to(m, (BLOCK, LANES)))
                m_all.append(jnp.concatenate(ms, axis=0))
            e_buf[slot, kh] = jnp.exp2(jnp.where(lanes_a, m_all[0], m_all[1]))

    def values_stage(j, slot):
        r0 = pl.multiple_of(j * BLOCK, BLOCK)
        qrows = pl.ds(r0, BLOCK)
        band = pl.ds(r0, 2 * BLOCK)
        for kh in range(N_KV_HEADS):
            o_full = [jnp.dot(p_buf[slot, kh, par], v_buf[kh, par, band, :], preferred_element_type=F32)
                      for par in range(2)]
            num = jnp.where(lanes_a, o_full[0], o_full[1])
            den = pltpu.roll(jnp.where(lanes_a, o_full[1], o_full[0]), LANE_GROUP, 1) + e_buf[slot, kh]
            out = (num / den).astype(BF16)
            for i in range(TILES_PER_KV_HEAD):
                att_buf[TILES_PER_KV_HEAD * kh + i, qrows, :] = out[i * BLOCK:(i + 1) * BLOCK]

    def block_body(j, carry):
        values_stage(j, j % 2)
        scores_stage(j + 1, (j + 1) % 2)
        return carry

    scores_stage(0, 0)
    lax.fori_loop(0, n_blocks - 1, block_body, 0)
    values_stage(n_blocks - 1, (n_blocks - 1) % 2)

    k_buf[:, :, :BLOCK, :] = k_buf[:, :, ATTN_TOKENS:ATTN_TOKENS + BLOCK, :]
    v_buf[:, :, :BLOCK, :] = v_buf[:, :, ATTN_TOKENS:ATTN_TOKENS + BLOCK, :]

    for r in range(ATTN_TOKENS // ATTN_OUT_ROWS):
        rs = slice(r * ATTN_OUT_ROWS, (r + 1) * ATTN_OUT_ROWS)
        att = jnp.concatenate([att_buf[p, rs, :] for p in range(N_HEAD_TILES)], axis=1)
        mix = jnp.dot(att, wo_ref[...], preferred_element_type=F32) + bo_ref[...]
        z = DEEPNORM_ALPHA * x_ref[0, rs, :] + mix
        o_ref[0, rs, :] = _layer_norm(z, g_ref[...], b_ref[...])


def _attn_ln(x, positions, wqkv, bqkv, sinks, wo, bo, g, b):
    bsz, seq, _ = x.shape
    n_tiles = seq // ATTN_TOKENS
    inv_freq = ROPE_THETA ** (-jnp.arange(0, ROT_DIM, 2, dtype=F32) / ROT_DIM)
    split = (D_MODEL, D_MODEL + KV_DIM)
    wqkv_l = jnp.concatenate([_interleave_heads(w) for w in jnp.split(wqkv, split, axis=1)], axis=1)
    bqkv_l = jnp.concatenate([_interleave_heads(w) for w in jnp.split(bqkv, split)])
    wo_l = _interleave_heads(wo.T).T
    return pl.pallas_call(
        _attn_ln_kernel,
        out_shape=jax.ShapeDtypeStruct(x.shape, F32),
        grid=(bsz, n_tiles),
        in_specs=[
            pl.BlockSpec((1, ATTN_TOKENS, D_MODEL), lambda i, t: (i, t, 0)),
            pl.BlockSpec((1, 1, 1, ATTN_TOKENS), lambda i, t: (i, t, 0, 0)),
            _resident((ROT_HALF, 1)),
            _resident((6 * ROT_HALF, 2 * LANES)),
            _resident((2, BLOCK, 2 * BLOCK)),
            _resident((D_MODEL, QKV_DIM)),
            _resident((1, QKV_DIM)),
            pl.BlockSpec(memory_space=pltpu.SMEM),
            _resident((D_MODEL, D_MODEL)),
            _resident((1, D_MODEL)),
            _resident((1, D_MODEL)),
            _resident((1, D_MODEL)),
        ],
        out_specs=pl.BlockSpec((1, ATTN_TOKENS, D_MODEL), lambda i, t: (i, t, 0)),
        scratch_shapes=[
            pltpu.VMEM((N_HEAD_TILES, ATTN_TOKENS, LANES), BF16),
            pltpu.VMEM((N_KV_HEADS, 2, BLOCK + ATTN_TOKENS, LANES), BF16),
            pltpu.VMEM((N_KV_HEADS, 2, BLOCK + ATTN_TOKENS, LANES), BF16),
            pltpu.VMEM((N_HEAD_TILES, ATTN_TOKENS, LANES), BF16),
            pltpu.VMEM((2, N_KV_HEADS, 2, TILES_PER_KV_HEAD * BLOCK, 2 * BLOCK), BF16),
            pltpu.VMEM((2, N_KV_HEADS, TILES_PER_KV_HEAD * BLOCK, LANES), F32),
        ],
        compiler_params=pltpu.CompilerParams(
            dimension_semantics=("arbitrary", "arbitrary"), vmem_limit_bytes=VMEM_LIMIT_BYTES),
        name="attn_ln",
    )(x, positions.reshape(bsz, n_tiles, 1, ATTN_TOKENS), inv_freq.reshape(ROT_HALF, 1),
      jnp.asarray(_rope_expand_matrix(), BF16), jnp.asarray(_band_bias()), wqkv_l.astype(BF16),
      bqkv_l.reshape(1, -1), sinks, wo_l.astype(BF16), bo.reshape(1, -1), g.reshape(1, -1), b.reshape(1, -1))


def kernel(x, positions, ln_g, ln_b, ffn_w_gate, ffn_w_up, ffn_w_down, pool_w, pool_b, pool_scale,
           attn_w_qkv, attn_b_qkv, attn_sinks, attn_w_o, attn_b_o):
    bsz, seq, d = x.shape
    assert seq % FFN_TOKENS == 0 and seq % ATTN_TOKENS == 0 and d == D_MODEL
    wg_all, wu_all, wd_all = ffn_w_gate.astype(BF16), ffn_w_up.astype(BF16), ffn_w_down.astype(BF16)

    def ffn(h, i, s, n, pool=None):
        out = _ffn_ln(h.reshape(bsz * seq, d), wg_all, wu_all, wd_all, i, s, ln_g[i, n], ln_b[i, n], pool, seq)
        return out.reshape(bsz, seq, d)

    for i in range(DEPTH):
        x = ffn(x, i, 0, 0)
        j = i // 2
        if i % 2 == 0:
            x = ffn(x, i, 1, 2, pool=(pool_w[j], pool_b[j], pool_scale[j], ln_g[i, 1], ln_b[i, 1]))
        else:
            x = _attn_ln(x, positions, attn_w_qkv[j], attn_b_qkv[j], attn_sinks[j], attn_w_o[j], attn_b_o[j],
                         ln_g[i, 1], ln_b[i, 1])
            x = ffn(x, i, 1, 2)
    return x
```

```python
import functools
import math

import jax
import jax.numpy as jnp
import numpy as np
from jax import lax
from jax.experimental import pallas as pl
from jax.experimental.pallas import tpu as pltpu

D_MODEL = 1024
D_FF = 2816
DEPTH = 2
POOL_WINDOWS = (2, 4, 8, 16)
POOL_GROUP_DIM = D_MODEL // len(POOL_WINDOWS)
MAX_POOL_WINDOW = max(POOL_WINDOWS)
HEAD_DIM = 64
N_Q_HEADS = D_MODEL // HEAD_DIM
N_KV_HEADS = 4
KV_DIM = N_KV_HEADS * HEAD_DIM
QKV_DIM = D_MODEL + 2 * KV_DIM
WINDOW = 128
BLOCK = 128
ROPE_THETA = 500000.0
ROT_DIM = HEAD_DIM // 4
ROT_HALF = ROT_DIM // 2
LN_EPS = 1e-5
DEEPNORM_ALPHA = (2 * DEPTH) ** 0.25
NEG_INF = -1e30
LOG2_E = math.log2(math.e)

LANES = 128
LANE_GROUP = 8
N_HEAD_TILES = N_Q_HEADS // 2
TILES_PER_KV_HEAD = N_HEAD_TILES // N_KV_HEADS

FFN_TOKENS = 1024
FFN_SUBTILE = 256
FFN_CHUNK = 256
W_STAGE_ROWS_UP = 128
W_STAGE_ROWS_DOWN = D_FF // 8
ATTN_TOKENS = 512
ATTN_OUT_ROWS = 256
VMEM_LIMIT_BYTES = 56 * 1024 * 1024

F32 = jnp.float32
BF16 = jnp.bfloat16


def _layer_norm(z, g, b):
    mu = jnp.mean(z, axis=-1, keepdims=True)
    zc = z - mu
    var = jnp.mean(zc * zc, axis=-1, keepdims=True)
    return zc * lax.rsqrt(var + LN_EPS) * g + b


def _resident(shape):
    return pl.BlockSpec(shape, lambda *_: (0,) * len(shape), pipeline_mode=pl.Buffered(1))


def _ffn_ln_body(src_ref, wg_ref, wu_ref, wd_ref, g_ref, b_ref, o_ref, h_ref, next_subtile_tasks=None):
    n_sub = FFN_TOKENS // FFN_SUBTILE
    for r in range(n_sub):
        rows = slice(r * FFN_SUBTILE, (r + 1) * FFN_SUBTILE)
        pending = next_subtile_tasks(r + 1) if next_subtile_tasks and r + 1 < n_sub else []
        x = src_ref[rows, :]
        xb = x.astype(BF16)
        for c in range(D_FF // FFN_CHUNK):
            sl = slice(c * FFN_CHUNK, (c + 1) * FFN_CHUNK)
            gate = jnp.dot(xb, wg_ref[:, sl], preferred_element_type=F32)
            up = jnp.dot(xb, wu_ref[:, sl], preferred_element_type=F32)
            h = gate * (1.0 / (1.0 + jnp.exp(-gate))) * up
            h_ref[rows, sl] = h.astype(BF16)
            if pending:
                pending.pop(0)()
        for task in pending:
            task()
        y = jnp.dot(h_ref[rows, :], wd_ref[...], preferred_element_type=F32)
        z = DEEPNORM_ALPHA * x + 0.5 * y
        o_ref[rows, :] = _layer_norm(z, g_ref[...], b_ref[...])


def _stage_weights_bf16(layer, slot, hbm_refs, vmem_refs, stage_refs, sem):
    chunks = []
    for hbm, vmem, stage in zip(hbm_refs, vmem_refs, stage_refs):
        rows = stage.shape[1]
        chunks += [(hbm, vmem, stage, k * rows, rows) for k in range(vmem.shape[0] // rows)]

    def copy(n):
        hbm, _, stage, row0, rows = chunks[n]
        return pltpu.make_async_copy(hbm.at[layer, slot, pl.ds(row0, rows), :], stage.at[n % 2], sem.at[n % 2])

    copy(0).start()
    for n, (_, vmem, stage, row0, rows) in enumerate(chunks):
        if n + 1 < len(chunks):
            copy(n + 1).start()
        copy(n).wait()
        vmem[row0:row0 + rows, :] = stage[n % 2].astype(BF16)


def _ffn_ln_kernel(x_ref, wg_hbm, wu_hbm, wd_hbm, g_ref, b_ref, o_ref, h_ref, wg_ref, wu_ref, wd_ref, up_stage,
                   down_stage, sem, *, layer, slot):
    @pl.when(pl.program_id(0) == 0)
    def _():
        _stage_weights_bf16(layer, slot, (wg_hbm, wu_hbm, wd_hbm), (wg_ref, wu_ref, wd_ref),
                            (up_stage, up_stage, down_stage), sem)

    _ffn_ln_body(x_ref, wg_ref, wu_ref, wd_ref, g_ref, b_ref, o_ref, h_ref)


def _pool_mixer_tasks(x_ref, halo_ref, r, seq_row0, pw_ref, pb_ref, ps_ref, g_ref, b_ref, mix_ref):
    rows = slice(r * FFN_SUBTILE, (r + 1) * FFN_SUBTILE)
    st = {}

    def load():
        st["x"] = x_ref[rows, :]
        if r == 0:
            st["prev"] = jnp.where(seq_row0 > 0, halo_ref[...], 0.0)
        else:
            st["prev"] = x_ref[r * FFN_SUBTILE - MAX_POOL_WINDOW:r * FFN_SUBTILE, :]
        st["pos"] = seq_row0 + r * FFN_SUBTILE + lax.broadcasted_iota(jnp.int32, (FFN_SUBTILE, 1), 0)
        st["ys"] = []

    def group(k):
        def run():
            win = POOL_WINDOWS[k]
            cols = slice(k * POOL_GROUP_DIM, (k + 1) * POOL_GROUP_DIM)
            s = jnp.concatenate([st["prev"][:, cols], st["x"][:, cols]], axis=0)
            shift = 1
            while shift < win:
                s = s + pltpu.roll(s, shift, 0)
                shift *= 2
            cnt = jnp.minimum(st["pos"] + 1, win).astype(F32)
            u = s[MAX_POOL_WINDOW:] / cnt - st["x"][:, cols]
            st["ys"].append(jnp.dot(u.astype(BF16), pw_ref[k], preferred_element_type=F32))
        return run

    def finish():
        y = (jnp.concatenate(st["ys"], axis=1) + pb_ref[...]) * ps_ref[...]
        z = DEEPNORM_ALPHA * st["x"] + y
        mix_ref[rows, :] = _layer_norm(z, g_ref[...], b_ref[...])

    return [load] + [group(k) for k in range(len(POOL_WINDOWS))] + [finish]


def _pool_ffn_ln_kernel(x_ref, halo_ref, pw_ref, pb_ref, ps_ref, pg_ref, pbeta_ref, wg_hbm, wu_hbm, wd_hbm,
                        g_ref, b_ref, o_ref, h_ref, wg_ref, wu_ref, wd_ref, up_stage, down_stage, sem, mix_ref,
                        *, layer, slot, seq):
    @pl.when(pl.program_id(0) == 0)
    def _():
        _stage_weights_bf16(layer, slot, (wg_hbm, wu_hbm, wd_hbm), (wg_ref, wu_ref, wd_ref),
                            (up_stage, up_stage, down_stage), sem)

    seq_row0 = (pl.program_id(0) * FFN_TOKENS) % seq

    def tasks(r):
        return _pool_mixer_tasks(x_ref, halo_ref, r, seq_row0, pw_ref, pb_ref, ps_ref, pg_ref, pbeta_ref, mix_ref)

    for task in tasks(0):
        task()
    _ffn_ln_body(mix_ref, wg_ref, wu_ref, wd_ref, g_ref, b_ref, o_ref, h_ref, next_subtile_tasks=tasks)


def _ffn_ln(x2d, wg_all, wu_all, wd_all, layer, slot, g, b, pool=None, seq=None):
    n_tok = x2d.shape[0]
    row = lambda i: (i, 0)
    vec = lambda v: v.reshape(1, -1)
    in_hbm = pl.BlockSpec(memory_space=pl.ANY)
    ffn_specs = [in_hbm, in_hbm, in_hbm, _resident((1, D_MODEL)), _resident((1, D_MODEL))]
    ffn_args = (wg_all, wu_all, wd_all, vec(g), vec(b))
    scratch = [
        pltpu.VMEM((FFN_TOKENS, D_FF), BF16),
        pltpu.VMEM((D_MODEL, D_FF), BF16),
        pltpu.VMEM((D_MODEL, D_FF), BF16),
        pltpu.VMEM((D_FF, D_MODEL), BF16),
        pltpu.VMEM((2, W_STAGE_ROWS_UP, D_FF), F32),
        pltpu.VMEM((2, W_STAGE_ROWS_DOWN, D_MODEL), F32),
        pltpu.SemaphoreType.DMA((2,)),
    ]
    if pool is None:
        body = functools.partial(_ffn_ln_kernel, layer=layer, slot=slot)
        in_specs, args = [pl.BlockSpec((FFN_TOKENS, D_MODEL), row)] + ffn_specs, (x2d,) + ffn_args
    else:
        pw, pb, ps, pg, pbeta = pool
        halo_blocks = FFN_TOKENS // MAX_POOL_WINDOW
        body = functools.partial(_pool_ffn_ln_kernel, layer=layer, slot=slot, seq=seq)
        in_specs = [
            pl.BlockSpec((FFN_TOKENS, D_MODEL), row),
            pl.BlockSpec((MAX_POOL_WINDOW, D_MODEL), lambda i: (jnp.maximum(i * halo_blocks - 1, 0), 0)),
            _resident((len(POOL_WINDOWS), POOL_GROUP_DIM, POOL_GROUP_DIM)),
            _resident((1, D_MODEL)),
            _resident((1, D_MODEL)),
            _resident((1, D_MODEL)),
            _resident((1, D_MODEL)),
        ] + ffn_specs
        args = (x2d, x2d, pw.astype(BF16), vec(pb), vec(ps), vec(pg), vec(pbeta)) + ffn_args
        scratch.append(pltpu.VMEM((FFN_TOKENS, D_MODEL), F32))
    return pl.pallas_call(
        body,
        out_shape=jax.ShapeDtypeStruct((n_tok, D_MODEL), F32),
        grid=(n_tok // FFN_TOKENS,),
        in_specs=in_specs,
        out_specs=pl.BlockSpec((FFN_TOKENS, D_MODEL), row),
        scratch_shapes=scratch,
        compiler_params=pltpu.CompilerParams(
            dimension_semantics=("arbitrary",), vmem_limit_bytes=VMEM_LIMIT_BYTES),
        name="ffn_ln" if pool is None else "pool_ffn_ln",
    )(*args)


def _interleave_heads(w):
    lead = w.shape[:-1]
    n_tiles = w.shape[-1] // LANES
    w = w.reshape(*lead, n_tiles, 2, HEAD_DIM // LANE_GROUP, LANE_GROUP)
    a = w.ndim - 2
    grp = lambda lo, hi: lax.slice_in_dim(w, lo, hi, axis=a)
    w = jnp.concatenate([grp(0, 1), grp(2, 5), grp(1, 2), grp(5, 8)], axis=a)
    w = jnp.swapaxes(w, a - 1, a)
    return w.reshape(*lead, n_tiles * LANES)


def _rope_expand_matrix():
    e = np.zeros((2 * ROT_HALF, 2 * LANES), np.float32)
    for lane in range(LANES):
        group, r = divmod(lane, LANE_GROUP)
        if group % (LANES // LANE_GROUP // 2) < 2:
            e[r, lane] = 1.0
            e[ROT_HALF + r, LANES + lane] = -1.0 if group < 2 else 1.0
    return np.concatenate([e, e, e], axis=0)


def _band_bias():
    qi = np.arange(BLOCK)[:, None] + BLOCK
    kj = np.arange(2 * BLOCK)[None, :]
    in_win = (qi - kj >= 0) & (qi - kj < WINDOW)
    first = in_win & (kj >= BLOCK)
    return np.where(np.stack([in_win, first]), 0.0, NEG_INF).astype(np.float32)


def _attn_ln_kernel(x_ref, pos_ref, freq_ref, rope_ref, bias_ref, wqkv_ref, bqkv_ref, sink_ref, wo_ref, bo_ref,
                    g_ref, b_ref, o_ref, q_buf, k_buf, v_buf, att_buf, p_buf, e_buf):
    t = pl.program_id(1)

    @pl.when(t == 0)
    def _():
        k_buf[:, :, :BLOCK, :] = jnp.zeros((N_KV_HEADS, 2, BLOCK, LANES), BF16)
        v_buf[:, :, :BLOCK, :] = jnp.zeros((N_KV_HEADS, 2, BLOCK, LANES), BF16)

    qkv = jnp.dot(x_ref[0].astype(BF16), wqkv_ref[...], preferred_element_type=F32) + bqkv_ref[...]

    def qkv_tile(col):
        return qkv[:, col:col + LANES]

    ang = freq_ref[...] * pos_ref[0, 0].astype(F32)
    trig = jnp.concatenate([jnp.cos(ang), jnp.sin(ang)], axis=0)
    hi = trig.astype(BF16)
    rest = trig - hi.astype(F32)
    mid = rest.astype(BF16)
    lo = (rest - mid.astype(F32)).astype(BF16)
    tables = lax.dot_general(jnp.concatenate([hi, mid, lo], axis=0), rope_ref[...], (((0,), (0,)), ((), ())),
                             preferred_element_type=F32)
    lane_group = lax.broadcasted_iota(jnp.int32, (1, LANES), 1) // LANE_GROUP
    lanes_a = lane_group % 2 == 0
    is_rot = lane_group % (LANES // LANE_GROUP // 2) < 2
    cos_t = tables[:, :LANES] + jnp.where(is_rot, 0.0, 1.0)
    sin_signed = tables[:, LANES:]

    def rotary(tile):
        return tile * cos_t + pltpu.roll(tile, LANES // 2, 1) * sin_signed

    scale = HEAD_DIM ** -0.5 * LOG2_E
    for p in range(N_HEAD_TILES):
        q_buf[p] = (rotary(qkv_tile(p * LANES)) * scale).astype(BF16)

    rows = slice(BLOCK, BLOCK + ATTN_TOKENS)
    for kh in range(N_KV_HEADS):
        c, par = kh // 2, kh % 2
        own = lanes_a if par == 0 else jnp.logical_not(lanes_a)
        to_other = LANE_GROUP if par == 0 else LANES - LANE_GROUP
        k_tile = rotary(qkv_tile(D_MODEL + c * LANES))
        v_tile = qkv_tile(D_MODEL + KV_DIM + c * LANES)
        k_own = jnp.where(own, k_tile, 0.0)
        v_moved = pltpu.roll(jnp.where(own, v_tile, 0.0), to_other, 1)
        k_buf[kh, par, rows, :] = k_own.astype(BF16)
        k_buf[kh, 1 - par, rows, :] = pltpu.roll(k_own, to_other, 1).astype(BF16)
        v_buf[kh, par, rows, :] = jnp.where(own, v_tile, 1.0).astype(BF16)
        v_buf[kh, 1 - par, rows, :] = jnp.where(own, 1.0, v_moved).astype(BF16)

    nt = (((1,), (1,)), ((), ()))
    n_blocks = ATTN_TOKENS // BLOCK

    def scores_stage(j, slot):
        r0 = pl.multiple_of(j * BLOCK, BLOCK)
        qrows = pl.ds(r0, BLOCK)
        band = pl.ds(r0, 2 * BLOCK)
        bias = bias_ref[jnp.where(t * n_blocks + j == 0, 1, 0)]
        for kh in range(N_KV_HEADS):
            tiles = [TILES_PER_KV_HEAD * kh + i for i in range(TILES_PER_KV_HEAD)]
            qq = jnp.concatenate([q_buf[p, qrows, :] for p in tiles], axis=0)
            m_all = []
            for par in range(2):
                s = lax.dot_general(qq, k_buf[kh, par, band, :], nt, preferred_element_type=F32)
                ms = []
                for i, p in enumerate(tiles):
                    sink = sink_ref[2 * p + par] * LOG2_E
                    sh = s[i * BLOCK:(i + 1) * BLOCK] + bias
                    m = jnp.maximum(jnp.max(sh, axis=-1, keepdims=True), sink)
                    p_buf[slot, kh, par, i * BLOCK:(i + 1) * BLOCK, :] = jnp.exp2(sh - m).astype(BF16)
                    ms.append(sink - jnp.broadcast_to(m, (BLOCK, LANES)))
                m_all.append(jnp.concatenate(ms, axis=0))
            e_buf[slot, kh] = jnp.exp2(jnp.where(lanes_a, m_all[0], m_all[1]))

    def values_stage(j, slot):
        r0 = pl.multiple_of(j * BLOCK, BLOCK)
        qrows = pl.ds(r0, BLOCK)
        band = pl.ds(r0, 2 * BLOCK)
        for kh in range(N_KV_HEADS):
            o_full = [jnp.dot(p_buf[slot, kh, par], v_buf[kh, par, band, :], preferred_element_type=F32)
                      for par in range(2)]
            num = jnp.where(lanes_a, o_full[0], o_full[1])
            den = pltpu.roll(jnp.where(lanes_a, o_full[1], o_full[0]), LANE_GROUP, 1) + e_buf[slot, kh]
            out = (num / den).astype(BF16)
            for i in range(TILES_PER_KV_HEAD):
                att_buf[TILES_PER_KV_HEAD * kh + i, qrows, :] = out[i * BLOCK:(i + 1) * BLOCK]

    def block_body(j, carry):
        values_stage(j, j % 2)
        scores_stage(j + 1, (j + 1) % 2)
        return carry

    scores_stage(0, 0)
    lax.fori_loop(0, n_blocks - 1, block_body, 0)
    values_stage(n_blocks - 1, (n_blocks - 1) % 2)

    k_buf[:, :, :BLOCK, :] = k_buf[:, :, ATTN_TOKENS:ATTN_TOKENS + BLOCK, :]
    v_buf[:, :, :BLOCK, :] = v_buf[:, :, ATTN_TOKENS:ATTN_TOKENS + BLOCK, :]

    for r in range(ATTN_TOKENS // ATTN_OUT_ROWS):
        rs = slice(r * ATTN_OUT_ROWS, (r + 1) * ATTN_OUT_ROWS)
        att = jnp.concatenate([att_buf[p, rs, :] for p in range(N_HEAD_TILES)], axis=1)
        mix = jnp.dot(att, wo_ref[...], preferred_element_type=F32) + bo_ref[...]
        z = DEEPNORM_ALPHA * x_ref[0, rs, :] + mix
        o_ref[0, rs, :] = _layer_norm(z, g_ref[...], b_ref[...])


def _attn_ln(x, positions, wqkv, bqkv, sinks, wo, bo, g, b):
    bsz, seq, _ = x.shape
    n_tiles = seq // ATTN_TOKENS
    inv_freq = ROPE_THETA ** (-jnp.arange(0, ROT_DIM, 2, dtype=F32) / ROT_DIM)
    split = (D_MODEL, D_MODEL + KV_DIM)
    wqkv_l = jnp.concatenate([_interleave_heads(w) for w in jnp.split(wqkv, split, axis=1)], axis=1)
    bqkv_l = jnp.concatenate([_interleave_heads(w) for w in jnp.split(bqkv, split)])
    wo_l = _interleave_heads(wo.T).T
    return pl.pallas_call(
        _attn_ln_kernel,
        out_shape=jax.ShapeDtypeStruct(x.shape, F32),
        grid=(bsz, n_tiles),
        in_specs=[
            pl.BlockSpec((1, ATTN_TOKENS, D_MODEL), lambda i, t: (i, t, 0)),
            pl.BlockSpec((1, 1, 1, ATTN_TOKENS), lambda i, t: (i, t, 0, 0)),
            _resident((ROT_HALF, 1)),
            _resident((6 * ROT_HALF, 2 * LANES)),
            _resident((2, BLOCK, 2 * BLOCK)),
            _resident((D_MODEL, QKV_DIM)),
            _resident((1, QKV_DIM)),
            pl.BlockSpec(memory_space=pltpu.SMEM),
            _resident((D_MODEL, D_MODEL)),
            _resident((1, D_MODEL)),
            _resident((1, D_MODEL)),
            _resident((1, D_MODEL)),
        ],
        out_specs=pl.BlockSpec((1, ATTN_TOKENS, D_MODEL), lambda i, t: (i, t, 0)),
        scratch_shapes=[
            pltpu.VMEM((N_HEAD_TILES, ATTN_TOKENS, LANES), BF16),
            pltpu.VMEM((N_KV_HEADS, 2, BLOCK + ATTN_TOKENS, LANES), BF16),
            pltpu.VMEM((N_KV_HEADS, 2, BLOCK + ATTN_TOKENS, LANES), BF16),
            pltpu.VMEM((N_HEAD_TILES, ATTN_TOKENS, LANES), BF16),
            pltpu.VMEM((2, N_KV_HEADS, 2, TILES_PER_KV_HEAD * BLOCK, 2 * BLOCK), BF16),
            pltpu.VMEM((2, N_KV_HEADS, TILES_PER_KV_HEAD * BLOCK, LANES), F32),
        ],
        compiler_params=pltpu.CompilerParams(
            dimension_semantics=("arbitrary", "arbitrary"), vmem_limit_bytes=VMEM_LIMIT_BYTES),
        name="attn_ln",
    )(x, positions.reshape(bsz, n_tiles, 1, ATTN_TOKENS), inv_freq.reshape(ROT_HALF, 1),
      jnp.asarray(_rope_expand_matrix(), BF16), jnp.asarray(_band_bias()), wqkv_l.astype(BF16),
      bqkv_l.reshape(1, -1), sinks, wo_l.astype(BF16), bo.reshape(1, -1), g.reshape(1, -1), b.reshape(1, -1))


def kernel(x, positions, ln_g, ln_b, ffn_w_gate, ffn_w_up, ffn_w_down, pool_w, pool_b, pool_scale,
           attn_w_qkv, attn_b_qkv, attn_sinks, attn_w_o, attn_b_o):
    bsz, seq, d = x.shape
    assert seq % FFN_TOKENS == 0 and seq % ATTN_TOKENS == 0 and d == D_MODEL

    def ffn(h, i, s, n, pool=None):
        out = _ffn_ln(h.reshape(bsz * seq, d), ffn_w_gate, ffn_w_up, ffn_w_down, i, s, ln_g[i, n], ln_b[i, n],
                      pool, seq)
        return out.reshape(bsz, seq, d)

    for i in range(DEPTH):
        x = ffn(x, i, 0, 0)
        j = i // 2
        if i % 2 == 0:
            x = ffn(x, i, 1, 2, pool=(pool_w[j], pool_b[j], pool_scale[j], ln_g[i, 1], ln_b[i, 1]))
        else:
            x = _attn_ln(x, positions, attn_w_qkv[j], attn_b_qkv[j], attn_sinks[j], attn_w_o[j], attn_b_o[j],
                         ln_g[i, 1], ln_b[i, 1])
            x = ffn(x, i, 1, 2)
    return x
```

```python
import functools
import math

import jax
import jax.numpy as jnp
import numpy as np
from jax import lax
from jax.experimental import pallas as pl
from jax.experimental.pallas import tpu as pltpu

D_MODEL = 1024
D_FF = 2816
DEPTH = 2
POOL_WINDOWS = (2, 4, 8, 16)
POOL_GROUP_DIM = D_MODEL // len(POOL_WINDOWS)
MAX_POOL_WINDOW = max(POOL_WINDOWS)
HEAD_DIM = 64
N_Q_HEADS = D_MODEL // HEAD_DIM
N_KV_HEADS = 4
KV_DIM = N_KV_HEADS * HEAD_DIM
QKV_DIM = D_MODEL + 2 * KV_DIM
WINDOW = 128
BLOCK = 128
ROPE_THETA = 500000.0
ROT_DIM = HEAD_DIM // 4
ROT_HALF = ROT_DIM // 2
LN_EPS = 1e-5
DEEPNORM_ALPHA = (2 * DEPTH) ** 0.25
NEG_INF = -1e30
LOG2_E = math.log2(math.e)

LANES = 128
LANE_GROUP = 8
N_HEAD_TILES = N_Q_HEADS // 2
TILES_PER_KV_HEAD = N_HEAD_TILES // N_KV_HEADS

FFN_TOKENS = 1024
FFN_SUBTILE = 256
FFN_CHUNK = 256
W_STAGE_SLOTS = 4
W_STAGE_ROWS_UP = 64
W_STAGE_ROWS_DOWN = D_FF // 16
ATTN_TOKENS = 512
ATTN_OUT_ROWS = 256
VMEM_LIMIT_BYTES = 56 * 1024 * 1024

F32 = jnp.float32
BF16 = jnp.bfloat16


def _layer_norm(z, g, b):
    mu = jnp.mean(z, axis=-1, keepdims=True)
    zc = z - mu
    var = jnp.mean(zc * zc, axis=-1, keepdims=True)
    return zc * lax.rsqrt(var + LN_EPS) * g + b


def _resident(shape):
    return pl.BlockSpec(shape, lambda *_: (0,) * len(shape), pipeline_mode=pl.Buffered(1))


def _ffn_ln_body(src_ref, wg_ref, wu_ref, wd_ref, g_ref, b_ref, o_ref, h_ref, next_subtile_tasks=None):
    n_sub = FFN_TOKENS // FFN_SUBTILE
    for r in range(n_sub):
        rows = slice(r * FFN_SUBTILE, (r + 1) * FFN_SUBTILE)
        pending = next_subtile_tasks(r + 1) if next_subtile_tasks and r + 1 < n_sub else []
        x = src_ref[rows, :]
        xb = x.astype(BF16)
        for c in range(D_FF // FFN_CHUNK):
            sl = slice(c * FFN_CHUNK, (c + 1) * FFN_CHUNK)
            gate = jnp.dot(xb, wg_ref[:, sl], preferred_element_type=F32)
            up = jnp.dot(xb, wu_ref[:, sl], preferred_element_type=F32)
            h = gate * (1.0 / (1.0 + jnp.exp(-gate))) * up
            h_ref[rows, sl] = h.astype(BF16)
            if pending:
                pending.pop(0)()
        for task in pending:
            task()
        y = jnp.dot(h_ref[rows, :], wd_ref[...], preferred_element_type=F32)
        z = DEEPNORM_ALPHA * x + 0.5 * y
        o_ref[rows, :] = _layer_norm(z, g_ref[...], b_ref[...])


def _stage_weights_bf16(layer, slot, hbm_refs, vmem_refs, stage_refs, sem):
    chunks = []
    for hbm, vmem, stage in zip(hbm_refs, vmem_refs, stage_refs):
        rows = stage.shape[1]
        chunks += [(hbm, vmem, stage, k * rows, rows) for k in range(vmem.shape[0] // rows)]

    def copy(n):
        hbm, _, stage, row0, rows = chunks[n]
        ring = n % W_STAGE_SLOTS
        return pltpu.make_async_copy(hbm.at[layer, slot, pl.ds(row0, rows), :], stage.at[ring], sem.at[ring])

    for n in range(W_STAGE_SLOTS - 1):
        copy(n).start()
    for n, (_, vmem, stage, row0, rows) in enumerate(chunks):
        if n + W_STAGE_SLOTS - 1 < len(chunks):
            copy(n + W_STAGE_SLOTS - 1).start()
        copy(n).wait()
        vmem[row0:row0 + rows, :] = stage[n % W_STAGE_SLOTS].astype(BF16)


def _ffn_ln_kernel(x_ref, wg_hbm, wu_hbm, wd_hbm, g_ref, b_ref, o_ref, h_ref, wg_ref, wu_ref, wd_ref, up_stage,
                   down_stage, sem, *, layer, slot):
    @pl.when(pl.program_id(0) == 0)
    def _():
        _stage_weights_bf16(layer, slot, (wg_hbm, wu_hbm, wd_hbm), (wg_ref, wu_ref, wd_ref),
                            (up_stage, up_stage, down_stage), sem)

    _ffn_ln_body(x_ref, wg_ref, wu_ref, wd_ref, g_ref, b_ref, o_ref, h_ref)


def _pool_mixer_tasks(x_ref, halo_ref, r, seq_row0, pw_ref, pb_ref, ps_ref, g_ref, b_ref, mix_ref):
    rows = slice(r * FFN_SUBTILE, (r + 1) * FFN_SUBTILE)
    st = {}

    def load():
        st["x"] = x_ref[rows, :]
        if r == 0:
            st["prev"] = jnp.where(seq_row0 > 0, halo_ref[...], 0.0)
        else:
            st["prev"] = x_ref[r * FFN_SUBTILE - MAX_POOL_WINDOW:r * FFN_SUBTILE, :]
        st["pos"] = seq_row0 + r * FFN_SUBTILE + lax.broadcasted_iota(jnp.int32, (FFN_SUBTILE, 1), 0)
        st["ys"] = []

    def group(k):
        def run():
            win = POOL_WINDOWS[k]
            cols = slice(k * POOL_GROUP_DIM, (k + 1) * POOL_GROUP_DIM)
            s = jnp.concatenate([st["prev"][:, cols], st["x"][:, cols]], axis=0)
            shift = 1
            while shift < win:
                s = s + pltpu.roll(s, shift, 0)
                shift *= 2
            cnt = jnp.minimum(st["pos"] + 1, win).astype(F32)
            u = s[MAX_POOL_WINDOW:] / cnt - st["x"][:, cols]
            st["ys"].append(jnp.dot(u.astype(BF16), pw_ref[k], preferred_element_type=F32))
        return run

    def finish():
        y = (jnp.concatenate(st["ys"], axis=1) + pb_ref[...]) * ps_ref[...]
        z = DEEPNORM_ALPHA * st["x"] + y
        mix_ref[rows, :] = _layer_norm(z, g_ref[...], b_ref[...])

    return [load] + [group(k) for k in range(len(POOL_WINDOWS))] + [finish]


def _pool_ffn_ln_kernel(x_ref, halo_ref, pw_ref, pb_ref, ps_ref, pg_ref, pbeta_ref, wg_hbm, wu_hbm, wd_hbm,
                        g_ref, b_ref, o_ref, h_ref, wg_ref, wu_ref, wd_ref, up_stage, down_stage, sem, mix_ref,
                        *, layer, slot, seq):
    @pl.when(pl.program_id(0) == 0)
    def _():
        _stage_weights_bf16(layer, slot, (wg_hbm, wu_hbm, wd_hbm), (wg_ref, wu_ref, wd_ref),
                            (up_stage, up_stage, down_stage), sem)

    seq_row0 = (pl.program_id(0) * FFN_TOKENS) % seq

    def tasks(r):
        return _pool_mixer_tasks(x_ref, halo_ref, r, seq_row0, pw_ref, pb_ref, ps_ref, pg_ref, pbeta_ref, mix_ref)

    for task in tasks(0):
        task()
    _ffn_ln_body(mix_ref, wg_ref, wu_ref, wd_ref, g_ref, b_ref, o_ref, h_ref, next_subtile_tasks=tasks)


def _ffn_ln(x2d, wg_all, wu_all, wd_all, layer, slot, g, b, pool=None, seq=None):
    n_tok = x2d.shape[0]
    row = lambda i: (i, 0)
    vec = lambda v: v.reshape(1, -1)
    in_hbm = pl.BlockSpec(memory_space=pl.ANY)
    ffn_specs = [in_hbm, in_hbm, in_hbm, _resident((1, D_MODEL)), _resident((1, D_MODEL))]
    ffn_args = (wg_all, wu_all, wd_all, vec(g), vec(b))
    scratch = [
        pltpu.VMEM((FFN_TOKENS, D_FF), BF16),
        pltpu.VMEM((D_MODEL, D_FF), BF16),
        pltpu.VMEM((D_MODEL, D_FF), BF16),
        pltpu.VMEM((D_FF, D_MODEL), BF16),
        pltpu.VMEM((W_STAGE_SLOTS, W_STAGE_ROWS_UP, D_FF), F32),
        pltpu.VMEM((W_STAGE_SLOTS, W_STAGE_ROWS_DOWN, D_MODEL), F32),
        pltpu.SemaphoreType.DMA((W_STAGE_SLOTS,)),
    ]
    if pool is None:
        body = functools.partial(_ffn_ln_kernel, layer=layer, slot=slot)
        in_specs, args = [pl.BlockSpec((FFN_TOKENS, D_MODEL), row)] + ffn_specs, (x2d,) + ffn_args
    else:
        pw, pb, ps, pg, pbeta = pool
        halo_blocks = FFN_TOKENS // MAX_POOL_WINDOW
        body = functools.partial(_pool_ffn_ln_kernel, layer=layer, slot=slot, seq=seq)
        in_specs = [
            pl.BlockSpec((FFN_TOKENS, D_MODEL), row),
            pl.BlockSpec((MAX_POOL_WINDOW, D_MODEL), lambda i: (jnp.maximum(i * halo_blocks - 1, 0), 0)),
            _resident((len(POOL_WINDOWS), POOL_GROUP_DIM, POOL_GROUP_DIM)),
            _resident((1, D_MODEL)),
            _resident((1, D_MODEL)),
            _resident((1, D_MODEL)),
            _resident((1, D_MODEL)),
        ] + ffn_specs
        args = (x2d, x2d, pw.astype(BF16), vec(pb), vec(ps), vec(pg), vec(pbeta)) + ffn_args
        scratch.append(pltpu.VMEM((FFN_TOKENS, D_MODEL), F32))
    return pl.pallas_call(
        body,
        out_shape=jax.ShapeDtypeStruct((n_tok, D_MODEL), F32),
        grid=(n_tok // FFN_TOKENS,),
        in_specs=in_specs,
        out_specs=pl.BlockSpec((FFN_TOKENS, D_MODEL), row),
        scratch_shapes=scratch,
        compiler_params=pltpu.CompilerParams(
            dimension_semantics=("arbitrary",), vmem_limit_bytes=VMEM_LIMIT_BYTES),
        name="ffn_ln" if pool is None else "pool_ffn_ln",
    )(*args)


def _interleave_heads(w):
    lead = w.shape[:-1]
    n_tiles = w.shape[-1] // LANES
    w = w.reshape(*lead, n_tiles, 2, HEAD_DIM // LANE_GROUP, LANE_GROUP)
    a = w.ndim - 2
    grp = lambda lo, hi: lax.slice_in_dim(w, lo, hi, axis=a)
    w = jnp.concatenate([grp(0, 1), grp(2, 5), grp(1, 2), grp(5, 8)], axis=a)
    w = jnp.swapaxes(w, a - 1, a)
    return w.reshape(*lead, n_tiles * LANES)


def _rope_expand_matrix():
    e = np.zeros((2 * ROT_HALF, 2 * LANES), np.float32)
    for lane in range(LANES):
        group, r = divmod(lane, LANE_GROUP)
        if group % (LANES // LANE_GROUP // 2) < 2:
            e[r, lane] = 1.0
            e[ROT_HALF + r, LANES + lane] = -1.0 if group < 2 else 1.0
    return np.concatenate([e, e, e], axis=0)


def _band_bias():
    qi = np.arange(BLOCK)[:, None] + BLOCK
    kj = np.arange(2 * BLOCK)[None, :]
    in_win = (qi - kj >= 0) & (qi - kj < WINDOW)
    first = in_win & (kj >= BLOCK)
    return np.where(np.stack([in_win, first]), 0.0, NEG_INF).astype(np.float32)


def _attn_ln_kernel(x_ref, pos_ref, freq_ref, rope_ref, bias_ref, wqkv_ref, bqkv_ref, sink_ref, wo_ref, bo_ref,
                    g_ref, b_ref, o_ref, q_buf, k_buf, v_buf, att_buf, p_buf, e_buf):
    t = pl.program_id(1)

    @pl.when(t == 0)
    def _():
        k_buf[:, :, :BLOCK, :] = jnp.zeros((N_KV_HEADS, 2, BLOCK, LANES), BF16)
        v_buf[:, :, :BLOCK, :] = jnp.zeros((N_KV_HEADS, 2, BLOCK, LANES), BF16)

    qkv = jnp.dot(x_ref[0].astype(BF16), wqkv_ref[...], preferred_element_type=F32) + bqkv_ref[...]

    def qkv_tile(col):
        return qkv[:, col:col + LANES]

    ang = freq_ref[...] * pos_ref[0, 0].astype(F32)
    trig = jnp.concatenate([jnp.cos(ang), jnp.sin(ang)], axis=0)
    hi = trig.astype(BF16)
    rest = trig - hi.astype(F32)
    mid = rest.astype(BF16)
    lo = (rest - mid.astype(F32)).astype(BF16)
    tables = lax.dot_general(jnp.concatenate([hi, mid, lo], axis=0), rope_ref[...], (((0,), (0,)), ((), ())),
                             preferred_element_type=F32)
    lane_group = lax.broadcasted_iota(jnp.int32, (1, LANES), 1) // LANE_GROUP
    lanes_a = lane_group % 2 == 0
    is_rot = lane_group % (LANES // LANE_GROUP // 2) < 2
    cos_t = tables[:, :LANES] + jnp.where(is_rot, 0.0, 1.0)
    sin_signed = tables[:, LANES:]

    def rotary(tile):
        return tile * cos_t + pltpu.roll(tile, LANES // 2, 1) * sin_signed

    scale = HEAD_DIM ** -0.5 * LOG2_E
    for p in range(N_HEAD_TILES):
        q_buf[p] = (rotary(qkv_tile(p * LANES)) * scale).astype(BF16)

    rows = slice(BLOCK, BLOCK + ATTN_TOKENS)
    for kh in range(N_KV_HEADS):
        c, par = kh // 2, kh % 2
        own = lanes_a if par == 0 else jnp.logical_not(lanes_a)
        to_other = LANE_GROUP if par == 0 else LANES - LANE_GROUP
        k_tile = rotary(qkv_tile(D_MODEL + c * LANES))
        v_tile = qkv_tile(D_MODEL + KV_DIM + c * LANES)
        k_own = jnp.where(own, k_tile, 0.0)
        v_moved = pltpu.roll(jnp.where(own, v_tile, 0.0), to_other, 1)
        k_buf[kh, par, rows, :] = k_own.astype(BF16)
        k_buf[kh, 1 - par, rows, :] = pltpu.roll(k_own, to_other, 1).astype(BF16)
        v_buf[kh, par, rows, :] = jnp.where(own, v_tile, 1.0).astype(BF16)
        v_buf[kh, 1 - par, rows, :] = jnp.where(own, 1.0, v_moved).astype(BF16)

    nt = (((1,), (1,)), ((), ()))
    n_blocks = ATTN_TOKENS // BLOCK

    def scores_stage(j, slot):
        r0 = pl.multiple_of(j * BLOCK, BLOCK)
        qrows = pl.ds(r0, BLOCK)
        band = pl.ds(r0, 2 * BLOCK)
        bias = bias_ref[jnp.where(t * n_blocks + j == 0, 1, 0)]
        for kh in range(N_KV_HEADS):
            tiles = [TILES_PER_KV_HEAD * kh + i for i in range(TILES_PER_KV_HEAD)]
            qq = jnp.concatenate([q_buf[p, qrows, :] for p in tiles], axis=0)
            m_all = []
            for par in range(2):
                s = lax.dot_general(qq, k_buf[kh, par, band, :], nt, preferred_element_type=F32)
                ms = []
                for i, p in enumerate(tiles):
                    sink = sink_ref[2 * p + par] * LOG2_E
                    sh = s[i * BLOCK:(i + 1) * BLOCK] + bias
                    m = jnp.maximum(jnp.max(sh, axis=-1, keepdims=True), sink)
                    p_buf[slot, kh, par, i * BLOCK:(i + 1) * BLOCK, :] = jnp.exp2(sh - m).astype(BF16)
                    ms.append(sink - jnp.broadcast_to(m, (BLOCK, LANES)))
                m_all.append(jnp.concatenate(ms, axis=0))
            e_buf[slot, kh] = jnp.exp2(jnp.where(lanes_a, m_all[0], m_all[1]))

    def values_stage(j, slot):
        r0 = pl.multiple_of(j * BLOCK, BLOCK)
        qrows = pl.ds(r0, BLOCK)
        band = pl.ds(r0, 2 * BLOCK)
        for kh in range(N_KV_HEADS):
            o_full = [jnp.dot(p_buf[slot, kh, par], v_buf[kh, par, band, :], preferred_element_type=F32)
                      for par in range(2)]
            num = jnp.where(lanes_a, o_full[0], o_full[1])
            den = pltpu.roll(jnp.where(lanes_a, o_full[1], o_full[0]), LANE_GROUP, 1) + e_buf[slot, kh]
            out = (num / den).astype(BF16)
            for i in range(TILES_PER_KV_HEAD):
                att_buf[TILES_PER_KV_HEAD * kh + i, qrows, :] = out[i * BLOCK:(i + 1) * BLOCK]

    def block_body(j, carry):
        values_stage(j, j % 2)
        scores_stage(j + 1, (j + 1) % 2)
        return carry

    scores_stage(0, 0)
    lax.fori_loop(0, n_blocks - 1, block_body, 0)
    values_stage(n_blocks - 1, (n_blocks - 1) % 2)

    k_buf[:, :, :BLOCK, :] = k_buf[:, :, ATTN_TOKENS:ATTN_TOKENS + BLOCK, :]
    v_buf[:, :, :BLOCK, :] = v_buf[:, :, ATTN_TOKENS:ATTN_TOKENS + BLOCK, :]

    for r in range(ATTN_TOKENS // ATTN_OUT_ROWS):
        rs = slice(r * ATTN_OUT_ROWS, (r + 1) * ATTN_OUT_ROWS)
        att = jnp.concatenate([att_buf[p, rs, :] for p in range(N_HEAD_TILES)], axis=1)
        mix = jnp.dot(att, wo_ref[...], preferred_element_type=F32) + bo_ref[...]
        z = DEEPNORM_ALPHA * x_ref[0, rs, :] + mix
        o_ref[0, rs, :] = _layer_norm(z, g_ref[...], b_ref[...])


def _attn_ln(x, positions, wqkv, bqkv, sinks, wo, bo, g, b):
    bsz, seq, _ = x.shape
    n_tiles = seq // ATTN_TOKENS
    inv_freq = ROPE_THETA ** (-jnp.arange(0, ROT_DIM, 2, dtype=F32) / ROT_DIM)
    split = (D_MODEL, D_MODEL + KV_DIM)
    wqkv_l = jnp.concatenate([_interleave_heads(w) for w in jnp.split(wqkv, split, axis=1)], axis=1)
    bqkv_l = jnp.concatenate([_interleave_heads(w) for w in jnp.split(bqkv, split)])
    wo_l = _interleave_heads(wo.T).T
    return pl.pallas_call(
        _attn_ln_kernel,
        out_shape=jax.ShapeDtypeStruct(x.shape, F32),
        grid=(bsz, n_tiles),
        in_specs=[
            pl.BlockSpec((1, ATTN_TOKENS, D_MODEL), lambda i, t: (i, t, 0)),
            pl.BlockSpec((1, 1, 1, ATTN_TOKENS), lambda i, t: (i, t, 0, 0)),
            _resident((ROT_HALF, 1)),
            _resident((6 * ROT_HALF, 2 * LANES)),
            _resident((2, BLOCK, 2 * BLOCK)),
            _resident((D_MODEL, QKV_DIM)),
            _resident((1, QKV_DIM)),
            pl.BlockSpec(memory_space=pltpu.SMEM),
            _resident((D_MODEL, D_MODEL)),
            _resident((1, D_MODEL)),
            _resident((1, D_MODEL)),
            _resident((1, D_MODEL)),
        ],
        out_specs=pl.BlockSpec((1, ATTN_TOKENS, D_MODEL), lambda i, t: (i, t, 0)),
        scratch_shapes=[
            pltpu.VMEM((N_HEAD_TILES, ATTN_TOKENS, LANES), BF16),
            pltpu.VMEM((N_KV_HEADS, 2, BLOCK + ATTN_TOKENS, LANES), BF16),
            pltpu.VMEM((N_KV_HEADS, 2, BLOCK + ATTN_TOKENS, LANES), BF16),
            pltpu.VMEM((N_HEAD_TILES, ATTN_TOKENS, LANES), BF16),
            pltpu.VMEM((2, N_KV_HEADS, 2, TILES_PER_KV_HEAD * BLOCK, 2 * BLOCK), BF16),
            pltpu.VMEM((2, N_KV_HEADS, TILES_PER_KV_HEAD * BLOCK, LANES), F32),
        ],
        compiler_params=pltpu.CompilerParams(
            dimension_semantics=("arbitrary", "arbitrary"), vmem_limit_bytes=VMEM_LIMIT_BYTES),
        name="attn_ln",
    )(x, positions.reshape(bsz, n_tiles, 1, ATTN_TOKENS), inv_freq.reshape(ROT_HALF, 1),
      jnp.asarray(_rope_expand_matrix(), BF16), jnp.asarray(_band_bias()), wqkv_l.astype(BF16),
      bqkv_l.reshape(1, -1), sinks, wo_l.astype(BF16), bo.reshape(1, -1), g.reshape(1, -1), b.reshape(1, -1))


def kernel(x, positions, ln_g, ln_b, ffn_w_gate, ffn_w_up, ffn_w_down, pool_w, pool_b, pool_scale,
           attn_w_qkv, attn_b_qkv, attn_sinks, attn_w_o, attn_b_o):
    bsz, seq, d = x.shape
    assert seq % FFN_TOKENS == 0 and seq % ATTN_TOKENS == 0 and d == D_MODEL

    def ffn(h, i, s, n, pool=None):
        out = _ffn_ln(h.reshape(bsz * seq, d), ffn_w_gate, ffn_w_up, ffn_w_down, i, s, ln_g[i, n], ln_b[i, n],
                      pool, seq)
        return out.reshape(bsz, seq, d)

    for i in range(DEPTH):
        x = ffn(x, i, 0, 0)
        j = i // 2
        if i % 2 == 0:
            x = ffn(x, i, 1, 2, pool=(pool_w[j], pool_b[j], pool_scale[j], ln_g[i, 1], ln_b[i, 1]))
        else:
            x = _attn_ln(x, positions, attn_w_qkv[j], attn_b_qkv[j], attn_sinks[j], attn_w_o[j], attn_b_o[j],
                         ln_g[i, 1], ln_b[i, 1])
            x = ffn(x, i, 1, 2)
    return x
```

```python
import functools
import math

import jax
import jax.numpy as jnp
import numpy as np
from jax import lax
from jax.experimental import pallas as pl
from jax.experimental.pallas import tpu as pltpu

D_MODEL = 1024
D_FF = 2816
DEPTH = 2
POOL_WINDOWS = (2, 4, 8, 16)
POOL_GROUP_DIM = D_MODEL // len(POOL_WINDOWS)
MAX_POOL_WINDOW = max(POOL_WINDOWS)
HEAD_DIM = 64
N_Q_HEADS = D_MODEL // HEAD_DIM
N_KV_HEADS = 4
KV_DIM = N_KV_HEADS * HEAD_DIM
QKV_DIM = D_MODEL + 2 * KV_DIM
WINDOW = 128
BLOCK = 128
ROPE_THETA = 500000.0
ROT_DIM = HEAD_DIM // 4
ROT_HALF = ROT_DIM // 2
LN_EPS = 1e-5
DEEPNORM_ALPHA = (2 * DEPTH) ** 0.25
NEG_INF = -1e30
LOG2_E = math.log2(math.e)

LANES = 128
LANE_GROUP = 8
N_HEAD_TILES = N_Q_HEADS // 2
TILES_PER_KV_HEAD = N_HEAD_TILES // N_KV_HEADS

FFN_TOKENS = 1024
FFN_SUBTILE = 256
FFN_CHUNK = 256
W_STAGE_SLOTS = 6
W_STAGE_ROWS_UP = 64
W_STAGE_ROWS_DOWN = D_FF // 16
ATTN_TOKENS = 512
ATTN_OUT_ROWS = 256
VMEM_LIMIT_BYTES = 56 * 1024 * 1024

F32 = jnp.float32
BF16 = jnp.bfloat16


def _layer_norm(z, g, b):
    mu = jnp.mean(z, axis=-1, keepdims=True)
    zc = z - mu
    var = jnp.mean(zc * zc, axis=-1, keepdims=True)
    return zc * lax.rsqrt(var + LN_EPS) * g + b


def _resident(shape):
    return pl.BlockSpec(shape, lambda *_: (0,) * len(shape), pipeline_mode=pl.Buffered(1))


def _ffn_ln_body(src_ref, wg_ref, wu_ref, wd_ref, g_ref, b_ref, o_ref, h_ref, next_subtile_tasks=None):
    n_sub = FFN_TOKENS // FFN_SUBTILE
    for r in range(n_sub):
        rows = slice(r * FFN_SUBTILE, (r + 1) * FFN_SUBTILE)
        pending = next_subtile_tasks(r + 1) if next_subtile_tasks and r + 1 < n_sub else []
        x = src_ref[rows, :]
        xb = x.astype(BF16)
        for c in range(D_FF // FFN_CHUNK):
            sl = slice(c * FFN_CHUNK, (c + 1) * FFN_CHUNK)
            gate = jnp.dot(xb, wg_ref[:, sl], preferred_element_type=F32)
            up = jnp.dot(xb, wu_ref[:, sl], preferred_element_type=F32)
            h = gate * (1.0 / (1.0 + jnp.exp(-gate))) * up
            h_ref[rows, sl] = h.astype(BF16)
            if pending:
                pending.pop(0)()
        for task in pending:
            task()
        y = jnp.dot(h_ref[rows, :], wd_ref[...], preferred_element_type=F32)
        z = DEEPNORM_ALPHA * x + 0.5 * y
        o_ref[rows, :] = _layer_norm(z, g_ref[...], b_ref[...])


def _stage_weights_bf16(layer, slot, hbm_refs, vmem_refs, stage_refs, sem):
    chunks = []
    for hbm, vmem, stage in zip(hbm_refs, vmem_refs, stage_refs):
        rows = stage.shape[1]
        chunks += [(hbm, vmem, stage, k * rows, rows) for k in range(vmem.shape[0] // rows)]

    def copy(n):
        hbm, _, stage, row0, rows = chunks[n]
        ring = n % W_STAGE_SLOTS
        return pltpu.make_async_copy(hbm.at[layer, slot, pl.ds(row0, rows), :], stage.at[ring], sem.at[ring])

    for n in range(W_STAGE_SLOTS - 1):
        copy(n).start()
    for n, (_, vmem, stage, row0, rows) in enumerate(chunks):
        if n + W_STAGE_SLOTS - 1 < len(chunks):
            copy(n + W_STAGE_SLOTS - 1).start()
        copy(n).wait()
        vmem[row0:row0 + rows, :] = stage[n % W_STAGE_SLOTS].astype(BF16)


def _ffn_ln_kernel(x_ref, wg_hbm, wu_hbm, wd_hbm, g_ref, b_ref, o_ref, h_ref, wg_ref, wu_ref, wd_ref, up_stage,
                   down_stage, sem, *, layer, slot):
    @pl.when(pl.program_id(0) == 0)
    def _():
        _stage_weights_bf16(layer, slot, (wg_hbm, wu_hbm, wd_hbm), (wg_ref, wu_ref, wd_ref),
                            (up_stage, up_stage, down_stage), sem)

    _ffn_ln_body(x_ref, wg_ref, wu_ref, wd_ref, g_ref, b_ref, o_ref, h_ref)


def _pool_mixer_tasks(x_ref, halo_ref, r, seq_row0, pw_ref, pb_ref, ps_ref, g_ref, b_ref, mix_ref):
    rows = slice(r * FFN_SUBTILE, (r + 1) * FFN_SUBTILE)
    st = {}

    def load():
        st["x"] = x_ref[rows, :]
        if r == 0:
            st["prev"] = jnp.where(seq_row0 > 0, halo_ref[...], 0.0)
        else:
            st["prev"] = x_ref[r * FFN_SUBTILE - MAX_POOL_WINDOW:r * FFN_SUBTILE, :]
        st["pos"] = seq_row0 + r * FFN_SUBTILE + lax.broadcasted_iota(jnp.int32, (FFN_SUBTILE, 1), 0)
        st["ys"] = []

    def group(k):
        def run():
            win = POOL_WINDOWS[k]
            cols = slice(k * POOL_GROUP_DIM, (k + 1) * POOL_GROUP_DIM)
            s = jnp.concatenate([st["prev"][:, cols], st["x"][:, cols]], axis=0)
            shift = 1
            while shift < win:
                s = s + pltpu.roll(s, shift, 0)
                shift *= 2
            cnt = jnp.minimum(st["pos"] + 1, win).astype(F32)
            u = s[MAX_POOL_WINDOW:] / cnt - st["x"][:, cols]
            st["ys"].append(jnp.dot(u.astype(BF16), pw_ref[k], preferred_element_type=F32))
        return run

    def finish():
        y = (jnp.concatenate(st["ys"], axis=1) + pb_ref[...]) * ps_ref[...]
        z = DEEPNORM_ALPHA * st["x"] + y
        mix_ref[rows, :] = _layer_norm(z, g_ref[...], b_ref[...])

    return [load] + [group(k) for k in range(len(POOL_WINDOWS))] + [finish]


def _pool_ffn_ln_kernel(x_ref, halo_ref, pw_ref, pb_ref, ps_ref, pg_ref, pbeta_ref, wg_hbm, wu_hbm, wd_hbm,
                        g_ref, b_ref, o_ref, h_ref, wg_ref, wu_ref, wd_ref, up_stage, down_stage, sem, mix_ref,
                        *, layer, slot, seq):
    @pl.when(pl.program_id(0) == 0)
    def _():
        _stage_weights_bf16(layer, slot, (wg_hbm, wu_hbm, wd_hbm), (wg_ref, wu_ref, wd_ref),
                            (up_stage, up_stage, down_stage), sem)

    seq_row0 = (pl.program_id(0) * FFN_TOKENS) % seq

    def tasks(r):
        return _pool_mixer_tasks(x_ref, halo_ref, r, seq_row0, pw_ref, pb_ref, ps_ref, pg_ref, pbeta_ref, mix_ref)

    for task in tasks(0):
        task()
    _ffn_ln_body(mix_ref, wg_ref, wu_ref, wd_ref, g_ref, b_ref, o_ref, h_ref, next_subtile_tasks=tasks)


def _ffn_ln(x2d, wg_all, wu_all, wd_all, layer, slot, g, b, pool=None, seq=None):
    n_tok = x2d.shape[0]
    row = lambda i: (i, 0)
    vec = lambda v: v.reshape(1, -1)
    in_hbm = pl.BlockSpec(memory_space=pl.ANY)
    ffn_specs = [in_hbm, in_hbm, in_hbm, _resident((1, D_MODEL)), _resident((1, D_MODEL))]
    ffn_args = (wg_all, wu_all, wd_all, vec(g), vec(b))
    scratch = [
        pltpu.VMEM((FFN_TOKENS, D_FF), BF16),
        pltpu.VMEM((D_MODEL, D_FF), BF16),
        pltpu.VMEM((D_MODEL, D_FF), BF16),
        pltpu.VMEM((D_FF, D_MODEL), BF16),
        pltpu.VMEM((W_STAGE_SLOTS, W_STAGE_ROWS_UP, D_FF), F32),
        pltpu.VMEM((W_STAGE_SLOTS, W_STAGE_ROWS_DOWN, D_MODEL), F32),
        pltpu.SemaphoreType.DMA((W_STAGE_SLOTS,)),
    ]
    if pool is None:
        body = functools.partial(_ffn_ln_kernel, layer=layer, slot=slot)
        in_specs, args = [pl.BlockSpec((FFN_TOKENS, D_MODEL), row)] + ffn_specs, (x2d,) + ffn_args
    else:
        pw, pb, ps, pg, pbeta = pool
        halo_blocks = FFN_TOKENS // MAX_POOL_WINDOW
        body = functools.partial(_pool_ffn_ln_kernel, layer=layer, slot=slot, seq=seq)
        in_specs = [
            pl.BlockSpec((FFN_TOKENS, D_MODEL), row),
            pl.BlockSpec((MAX_POOL_WINDOW, D_MODEL), lambda i: (jnp.maximum(i * halo_blocks - 1, 0), 0)),
            _resident((len(POOL_WINDOWS), POOL_GROUP_DIM, POOL_GROUP_DIM)),
            _resident((1, D_MODEL)),
            _resident((1, D_MODEL)),
            _resident((1, D_MODEL)),
            _resident((1, D_MODEL)),
        ] + ffn_specs
        args = (x2d, x2d, pw.astype(BF16), vec(pb), vec(ps), vec(pg), vec(pbeta)) + ffn_args
        scratch.append(pltpu.VMEM((FFN_TOKENS, D_MODEL), F32))
    return pl.pallas_call(
        body,
        out_shape=jax.ShapeDtypeStruct((n_tok, D_MODEL), F32),
        grid=(n_tok // FFN_TOKENS,),
        in_specs=in_specs,
        out_specs=pl.BlockSpec((FFN_TOKENS, D_MODEL), row),
        scratch_shapes=scratch,
        compiler_params=pltpu.CompilerParams(
            dimension_semantics=("arbitrary",), vmem_limit_bytes=VMEM_LIMIT_BYTES),
        name="ffn_ln" if pool is None else "pool_ffn_ln",
    )(*args)


def _interleave_heads(w):
    lead = w.shape[:-1]
    n_tiles = w.shape[-1] // LANES
    w = w.reshape(*lead, n_tiles, 2, HEAD_DIM // LANE_GROUP, LANE_GROUP)
    a = w.ndim - 2
    grp = lambda lo, hi: lax.slice_in_dim(w, lo, hi, axis=a)
    w = jnp.concatenate([grp(0, 1), grp(2, 5), grp(1, 2), grp(5, 8)], axis=a)
    w = jnp.swapaxes(w, a - 1, a)
    return w.reshape(*lead, n_tiles * LANES)


def _rope_expand_matrix():
    e = np.zeros((2 * ROT_HALF, 2 * LANES), np.float32)
    for lane in range(LANES):
        group, r = divmod(lane, LANE_GROUP)
        if group % (LANES // LANE_GROUP // 2) < 2:
            e[r, lane] = 1.0
            e[ROT_HALF + r, LANES + lane] = -1.0 if group < 2 else 1.0
    return np.concatenate([e, e, e], axis=0)


def _band_bias():
    qi = np.arange(BLOCK)[:, None] + BLOCK
    kj = np.arange(2 * BLOCK)[None, :]
    in_win = (qi - kj >= 0) & (qi - kj < WINDOW)
    first = in_win & (kj >= BLOCK)
    return np.where(np.stack([in_win, first]), 0.0, NEG_INF).astype(np.float32)


def _attn_ln_kernel(x_ref, pos_ref, freq_ref, rope_ref, bias_ref, wqkv_ref, bqkv_ref, sink_ref, wo_ref, bo_ref,
                    g_ref, b_ref, o_ref, q_buf, k_buf, v_buf, att_buf, p_buf, e_buf):
    t = pl.program_id(1)

    @pl.when(t == 0)
    def _():
        k_buf[:, :, :BLOCK, :] = jnp.zeros((N_KV_HEADS, 2, BLOCK, LANES), BF16)
        v_buf[:, :, :BLOCK, :] = jnp.zeros((N_KV_HEADS, 2, BLOCK, LANES), BF16)

    qkv = jnp.dot(x_ref[0].astype(BF16), wqkv_ref[...], preferred_element_type=F32) + bqkv_ref[...]

    def qkv_tile(col):
        return qkv[:, col:col + LANES]

    ang = freq_ref[...] * pos_ref[0, 0].astype(F32)
    trig = jnp.concatenate([jnp.cos(ang), jnp.sin(ang)], axis=0)
    hi = trig.astype(BF16)
    rest = trig - hi.astype(F32)
    mid = rest.astype(BF16)
    lo = (rest - mid.astype(F32)).astype(BF16)
    tables = lax.dot_general(jnp.concatenate([hi, mid, lo], axis=0), rope_ref[...], (((0,), (0,)), ((), ())),
                             preferred_element_type=F32)
    lane_group = lax.broadcasted_iota(jnp.int32, (1, LANES), 1) // LANE_GROUP
    lanes_a = lane_group % 2 == 0
    is_rot = lane_group % (LANES // LANE_GROUP // 2) < 2
    cos_t = tables[:, :LANES] + jnp.where(is_rot, 0.0, 1.0)
    sin_signed = tables[:, LANES:]

    def rotary(tile):
        return tile * cos_t + pltpu.roll(tile, LANES // 2, 1) * sin_signed

    scale = HEAD_DIM ** -0.5 * LOG2_E
    for p in range(N_HEAD_TILES):
        q_buf[p] = (rotary(qkv_tile(p * LANES)) * scale).astype(BF16)

    rows = slice(BLOCK, BLOCK + ATTN_TOKENS)
    for kh in range(N_KV_HEADS):
        c, par = kh // 2, kh % 2
        own = lanes_a if par == 0 else jnp.logical_not(lanes_a)
        to_other = LANE_GROUP if par == 0 else LANES - LANE_GROUP
        k_tile = rotary(qkv_tile(D_MODEL + c * LANES))
        v_tile = qkv_tile(D_MODEL + KV_DIM + c * LANES)
        k_own = jnp.where(own, k_tile, 0.0)
        v_moved = pltpu.roll(jnp.where(own, v_tile, 0.0), to_other, 1)
        k_buf[kh, par, rows, :] = k_own.astype(BF16)
        k_buf[kh, 1 - par, rows, :] = pltpu.roll(k_own, to_other, 1).astype(BF16)
        v_buf[kh, par, rows, :] = jnp.where(own, v_tile, 1.0).astype(BF16)
        v_buf[kh, 1 - par, rows, :] = jnp.where(own, 1.0, v_moved).astype(BF16)

    nt = (((1,), (1,)), ((), ()))
    n_blocks = ATTN_TOKENS // BLOCK

    def scores_stage(j, slot):
        r0 = pl.multiple_of(j * BLOCK, BLOCK)
        qrows = pl.ds(r0, BLOCK)
        band = pl.ds(r0, 2 * BLOCK)
        bias = bias_ref[jnp.where(t * n_blocks + j == 0, 1, 0)]
        for kh in range(N_KV_HEADS):
            tiles = [TILES_PER_KV_HEAD * kh + i for i in range(TILES_PER_KV_HEAD)]
            qq = jnp.concatenate([q_buf[p, qrows, :] for p in tiles], axis=0)
            m_all = []
            for par in range(2):
                s = lax.dot_general(qq, k_buf[kh, par, band, :], nt, preferred_element_type=F32)
                ms = []
                for i, p in enumerate(tiles):
                    sink = sink_ref[2 * p + par] * LOG2_E
                    sh = s[i * BLOCK:(i + 1) * BLOCK] + bias
                    m = jnp.maximum(jnp.max(sh, axis=-1, keepdims=True), sink)
                    p_buf[slot, kh, par, i * BLOCK:(i + 1) * BLOCK, :] = jnp.exp2(sh - m).astype(BF16)
                    ms.append(sink - jnp.broadcast_to(m, (BLOCK, LANES)))
                m_all.append(jnp.concatenate(ms, axis=0))
            e_buf[slot, kh] = jnp.exp2(jnp.where(lanes_a, m_all[0], m_all[1]))

    def values_stage(j, slot):
        r0 = pl.multiple_of(j * BLOCK, BLOCK)
        qrows = pl.ds(r0, BLOCK)
        band = pl.ds(r0, 2 * BLOCK)
        for kh in range(N_KV_HEADS):
            o_full = [jnp.dot(p_buf[slot, kh, par], v_buf[kh, par, band, :], preferred_element_type=F32)
                      for par in range(2)]
            num = jnp.where(lanes_a, o_full[0], o_full[1])
            den = pltpu.roll(jnp.where(lanes_a, o_full[1], o_full[0]), LANE_GROUP, 1) + e_buf[slot, kh]
            out = (num / den).astype(BF16)
            for i in range(TILES_PER_KV_HEAD):
                att_buf[TILES_PER_KV_HEAD * kh + i, qrows, :] = out[i * BLOCK:(i + 1) * BLOCK]

    def block_body(j, carry):
        values_stage(j, j % 2)
        scores_stage(j + 1, (j + 1) % 2)
        return carry

    scores_stage(0, 0)
    lax.fori_loop(0, n_blocks - 1, block_body, 0)
    values_stage(n_blocks - 1, (n_blocks - 1) % 2)

    k_buf[:, :, :BLOCK, :] = k_buf[:, :, ATTN_TOKENS:ATTN_TOKENS + BLOCK, :]
    v_buf[:, :, :BLOCK, :] = v_buf[:, :, ATTN_TOKENS:ATTN_TOKENS + BLOCK, :]

    for r in range(ATTN_TOKENS // ATTN_OUT_ROWS):
        rs = slice(r * ATTN_OUT_ROWS, (r + 1) * ATTN_OUT_ROWS)
        att = jnp.concatenate([att_buf[p, rs, :] for p in range(N_HEAD_TILES)], axis=1)
        mix = jnp.dot(att, wo_ref[...], preferred_element_type=F32) + bo_ref[...]
        z = DEEPNORM_ALPHA * x_ref[0, rs, :] + mix
        o_ref[0, rs, :] = _layer_norm(z, g_ref[...], b_ref[...])


def _attn_ln(x, positions, wqkv, bqkv, sinks, wo, bo, g, b):
    bsz, seq, _ = x.shape
    n_tiles = seq // ATTN_TOKENS
    inv_freq = ROPE_THETA ** (-jnp.arange(0, ROT_DIM, 2, dtype=F32) / ROT_DIM)
    split = (D_MODEL, D_MODEL + KV_DIM)
    wqkv_l = jnp.concatenate([_interleave_heads(w) for w in jnp.split(wqkv, split, axis=1)], axis=1)
    bqkv_l = jnp.concatenate([_interleave_heads(w) for w in jnp.split(bqkv, split)])
    wo_l = _interleave_heads(wo.T).T
    return pl.pallas_call(
        _attn_ln_kernel,
        out_shape=jax.ShapeDtypeStruct(x.shape, F32),
        grid=(bsz, n_tiles),
        in_specs=[
            pl.BlockSpec((1, ATTN_TOKENS, D_MODEL), lambda i, t: (i, t, 0)),
            pl.BlockSpec((1, 1, 1, ATTN_TOKENS), lambda i, t: (i, t, 0, 0)),
            _resident((ROT_HALF, 1)),
            _resident((6 * ROT_HALF, 2 * LANES)),
            _resident((2, BLOCK, 2 * BLOCK)),
            _resident((D_MODEL, QKV_DIM)),
            _resident((1, QKV_DIM)),
            pl.BlockSpec(memory_space=pltpu.SMEM),
            _resident((D_MODEL, D_MODEL)),
            _resident((1, D_MODEL)),
            _resident((1, D_MODEL)),
            _resident((1, D_MODEL)),
        ],
        out_specs=pl.BlockSpec((1, ATTN_TOKENS, D_MODEL), lambda i, t: (i, t, 0)),
        scratch_shapes=[
            pltpu.VMEM((N_HEAD_TILES, ATTN_TOKENS, LANES), BF16),
            pltpu.VMEM((N_KV_HEADS, 2, BLOCK + ATTN_TOKENS, LANES), BF16),
            pltpu.VMEM((N_KV_HEADS, 2, BLOCK + ATTN_TOKENS, LANES), BF16),
            pltpu.VMEM((N_HEAD_TILES, ATTN_TOKENS, LANES), BF16),
            pltpu.VMEM((2, N_KV_HEADS, 2, TILES_PER_KV_HEAD * BLOCK, 2 * BLOCK), BF16),
            pltpu.VMEM((2, N_KV_HEADS, TILES_PER_KV_HEAD * BLOCK, LANES), F32),
        ],
        compiler_params=pltpu.CompilerParams(
            dimension_semantics=("arbitrary", "arbitrary"), vmem_limit_bytes=VMEM_LIMIT_BYTES),
        name="attn_ln",
    )(x, positions.reshape(bsz, n_tiles, 1, ATTN_TOKENS), inv_freq.reshape(ROT_HALF, 1),
      jnp.asarray(_rope_expand_matrix(), BF16), jnp.asarray(_band_bias()), wqkv_l.astype(BF16),
      bqkv_l.reshape(1, -1), sinks, wo_l.astype(BF16), bo.reshape(1, -1), g.reshape(1, -1), b.reshape(1, -1))


def kernel(x, positions, ln_g, ln_b, ffn_w_gate, ffn_w_up, ffn_w_down, pool_w, pool_b, pool_scale,
           attn_w_qkv, attn_b_qkv, attn_sinks, attn_w_o, attn_b_o):
    bsz, seq, d = x.shape
    assert seq % FFN_TOKENS == 0 and seq % ATTN_TOKENS == 0 and d == D_MODEL

    def ffn(h, i, s, n, pool=None):
        out = _ffn_ln(h.reshape(bsz * seq, d), ffn_w_gate, ffn_w_up, ffn_w_down, i, s, ln_g[i, n], ln_b[i, n],
                      pool, seq)
        return out.reshape(bsz, seq, d)

    for i in range(DEPTH):
        x = ffn(x, i, 0, 0)
        j = i // 2
        if i % 2 == 0:
            x = ffn(x, i, 1, 2, pool=(pool_w[j], pool_b[j], pool_scale[j], ln_g[i, 1], ln_b[i, 1]))
        else:
            x = _attn_ln(x, positions, attn_w_qkv[j], attn_b_qkv[j], attn_sinks[j], attn_w_o[j], attn_b_o[j],
                         ln_g[i, 1], ln_b[i, 1])
            x = ffn(x, i, 1, 2)
    return x
```

```python
import functools
import math

import jax
import jax.numpy as jnp
import numpy as np
from jax import lax
from jax.experimental import pallas as pl
from jax.experimental.pallas import tpu as pltpu

D_MODEL = 1024
D_FF = 2816
DEPTH = 2
POOL_WINDOWS = (2, 4, 8, 16)
POOL_GROUP_DIM = D_MODEL // len(POOL_WINDOWS)
MAX_POOL_WINDOW = max(POOL_WINDOWS)
HEAD_DIM = 64
N_Q_HEADS = D_MODEL // HEAD_DIM
N_KV_HEADS = 4
KV_DIM = N_KV_HEADS * HEAD_DIM
QKV_DIM = D_MODEL + 2 * KV_DIM
WINDOW = 128
BLOCK = 128
ROPE_THETA = 500000.0
ROT_DIM = HEAD_DIM // 4
ROT_HALF = ROT_DIM // 2
LN_EPS = 1e-5
DEEPNORM_ALPHA = (2 * DEPTH) ** 0.25
NEG_INF = -1e30
LOG2_E = math.log2(math.e)

LANES = 128
LANE_GROUP = 8
N_HEAD_TILES = N_Q_HEADS // 2
TILES_PER_KV_HEAD = N_HEAD_TILES // N_KV_HEADS

FFN_TOKENS = 1024
FFN_SUBTILE = 256
FFN_CHUNK = 256
W_STAGE_SLOTS = 6
W_STAGE_ROWS_UP = 64
W_STAGE_ROWS_DOWN = D_FF // 16
ATTN_TOKENS = 512
ATTN_OUT_ROWS = 256
VMEM_LIMIT_BYTES = 56 * 1024 * 1024

F32 = jnp.float32
BF16 = jnp.bfloat16


def _layer_norm(z, g, b):
    mu = jnp.mean(z, axis=-1, keepdims=True)
    zc = z - mu
    var = jnp.mean(zc * zc, axis=-1, keepdims=True)
    return zc * lax.rsqrt(var + LN_EPS) * g + b


def _resident(shape):
    return pl.BlockSpec(shape, lambda *_: (0,) * len(shape), pipeline_mode=pl.Buffered(1))


def _ffn_ln_body(src_ref, wg_ref, wu_ref, wd_ref, g_ref, b_ref, o_ref, h_ref, next_subtile_tasks=None):
    n_sub = FFN_TOKENS // FFN_SUBTILE
    for r in range(n_sub):
        rows = slice(r * FFN_SUBTILE, (r + 1) * FFN_SUBTILE)
        pending = next_subtile_tasks(r + 1) if next_subtile_tasks and r + 1 < n_sub else []
        x = src_ref[rows, :]
        xb = x.astype(BF16)
        for c in range(D_FF // FFN_CHUNK):
            sl = slice(c * FFN_CHUNK, (c + 1) * FFN_CHUNK)
            gate = jnp.dot(xb, wg_ref[:, sl], preferred_element_type=F32)
            up = jnp.dot(xb, wu_ref[:, sl], preferred_element_type=F32)
            h = gate * (1.0 / (1.0 + jnp.exp(-gate))) * up
            h_ref[rows, sl] = h.astype(BF16)
            if pending:
                pending.pop(0)()
        for task in pending:
            task()
        y = jnp.dot(h_ref[rows, :], wd_ref[...], preferred_element_type=F32)
        z = DEEPNORM_ALPHA * x + 0.5 * y
        o_ref[rows, :] = _layer_norm(z, g_ref[...], b_ref[...])


def _stage_weights_bf16(layer, slot, hbm_refs, vmem_refs, stage_refs, sem):
    chunks = []
    for hbm, vmem, stage in zip(hbm_refs, vmem_refs, stage_refs):
        rows = stage.shape[1]
        chunks += [(hbm, vmem, stage, k * rows, rows) for k in range(vmem.shape[0] // rows)]

    def copy(n):
        hbm, _, stage, row0, rows = chunks[n]
        ring = n % W_STAGE_SLOTS
        return pltpu.make_async_copy(hbm.at[layer, slot, pl.ds(row0, rows), :], stage.at[ring], sem.at[ring])

    for n in range(W_STAGE_SLOTS - 1):
        copy(n).start()
    for n, (_, vmem, stage, row0, rows) in enumerate(chunks):
        if n + W_STAGE_SLOTS - 1 < len(chunks):
            copy(n + W_STAGE_SLOTS - 1).start()
        copy(n).wait()
        vmem[row0:row0 + rows, :] = stage[n % W_STAGE_SLOTS].astype(BF16)


def _ffn_ln_kernel(x_ref, wg_hbm, wu_hbm, wd_hbm, g_ref, b_ref, o_ref, h_ref, wg_ref, wu_ref, wd_ref, up_stage,
                   down_stage, sem, *, layer, slot):
    @pl.when(pl.program_id(0) == 0)
    def _():
        _stage_weights_bf16(layer, slot, (wg_hbm, wu_hbm, wd_hbm), (wg_ref, wu_ref, wd_ref),
                            (up_stage, up_stage, down_stage), sem)

    _ffn_ln_body(x_ref, wg_ref, wu_ref, wd_ref, g_ref, b_ref, o_ref, h_ref)


def _pool_mixer_tasks(x_ref, halo_ref, r, seq_row0, pw_ref, pb_ref, ps_ref, g_ref, b_ref, mix_ref):
    rows = slice(r * FFN_SUBTILE, (r + 1) * FFN_SUBTILE)
    st = {}

    def load():
        st["x"] = x_ref[rows, :]
        if r == 0:
            st["prev"] = jnp.where(seq_row0 > 0, halo_ref[...], 0.0)
        else:
            st["prev"] = x_ref[r * FFN_SUBTILE - MAX_POOL_WINDOW:r * FFN_SUBTILE, :]
        st["pos"] = seq_row0 + r * FFN_SUBTILE + lax.broadcasted_iota(jnp.int32, (FFN_SUBTILE, 1), 0)
        st["ys"] = []

    def group(k):
        def run():
            win = POOL_WINDOWS[k]
            cols = slice(k * POOL_GROUP_DIM, (k + 1) * POOL_GROUP_DIM)
            s = jnp.concatenate([st["prev"][:, cols], st["x"][:, cols]], axis=0)
            shift = 1
            while shift < win:
                s = s + pltpu.roll(s, shift, 0)
                shift *= 2
            cnt = jnp.minimum(st["pos"] + 1, win).astype(F32)
            u = s[MAX_POOL_WINDOW:] / cnt - st["x"][:, cols]
            st["ys"].append(jnp.dot(u.astype(BF16), pw_ref[k], preferred_element_type=F32))
        return run

    def finish():
        y = (jnp.concatenate(st["ys"], axis=1) + pb_ref[...]) * ps_ref[...]
        z = DEEPNORM_ALPHA * st["x"] + y
        mix_ref[rows, :] = _layer_norm(z, g_ref[...], b_ref[...])

    return [load] + [group(k) for k in range(len(POOL_WINDOWS))] + [finish]


def _pool_ffn_ln_kernel(x_ref, halo_ref, pw_ref, pb_ref, ps_ref, pg_ref, pbeta_ref, wg_hbm, wu_hbm, wd_hbm,
                        g_ref, b_ref, o_ref, mix_ref, h_ref, wg_ref, wu_ref, wd_ref, up_stage, down_stage, sem,
                        *, layer, slot, seq):
    @pl.when(pl.program_id(0) == 0)
    def _():
        _stage_weights_bf16(layer, slot, (wg_hbm, wu_hbm, wd_hbm), (wg_ref, wu_ref, wd_ref),
                            (up_stage, up_stage, down_stage), sem)

    seq_row0 = (pl.program_id(0) * FFN_TOKENS) % seq

    def tasks(r):
        return _pool_mixer_tasks(x_ref, halo_ref, r, seq_row0, pw_ref, pb_ref, ps_ref, pg_ref, pbeta_ref, mix_ref)

    for task in tasks(0):
        task()
    _ffn_ln_body(mix_ref, wg_ref, wu_ref, wd_ref, g_ref, b_ref, o_ref, h_ref, next_subtile_tasks=tasks)


def _ffn_ln(x2d, wg_all, wu_all, wd_all, layer, slot, g, b, pool=None, seq=None):
    n_tok = x2d.shape[0]
    row = lambda i: (i, 0)
    vec = lambda v: v.reshape(1, -1)
    in_hbm = pl.BlockSpec(memory_space=pl.ANY)
    ffn_specs = [in_hbm, in_hbm, in_hbm, _resident((1, D_MODEL)), _resident((1, D_MODEL))]
    ffn_args = (wg_all, wu_all, wd_all, vec(g), vec(b))
    scratch = [
        pltpu.VMEM((FFN_TOKENS, D_FF), BF16),
        pltpu.VMEM((D_MODEL, D_FF), BF16),
        pltpu.VMEM((D_MODEL, D_FF), BF16),
        pltpu.VMEM((D_FF, D_MODEL), BF16),
        pltpu.VMEM((W_STAGE_SLOTS, W_STAGE_ROWS_UP, D_FF), F32),
        pltpu.VMEM((W_STAGE_SLOTS, W_STAGE_ROWS_DOWN, D_MODEL), F32),
        pltpu.SemaphoreType.DMA((W_STAGE_SLOTS,)),
    ]
    if pool is None:
        body = functools.partial(_ffn_ln_kernel, layer=layer, slot=slot)
        in_specs, args = [pl.BlockSpec((FFN_TOKENS, D_MODEL), row)] + ffn_specs, (x2d,) + ffn_args
    else:
        pw, pb, ps, pg, pbeta = pool
        halo_blocks = FFN_TOKENS // MAX_POOL_WINDOW
        body = functools.partial(_pool_ffn_ln_kernel, layer=layer, slot=slot, seq=seq)
        in_specs = [
            pl.BlockSpec((FFN_TOKENS, D_MODEL), row),
            pl.BlockSpec((MAX_POOL_WINDOW, D_MODEL), lambda i: (jnp.maximum(i * halo_blocks - 1, 0), 0)),
            _resident((len(POOL_WINDOWS), POOL_GROUP_DIM, POOL_GROUP_DIM)),
            _resident((1, D_MODEL)),
            _resident((1, D_MODEL)),
            _resident((1, D_MODEL)),
            _resident((1, D_MODEL)),
        ] + ffn_specs
        args = (x2d, x2d, pw.astype(BF16), vec(pb), vec(ps), vec(pg), vec(pbeta)) + ffn_args
        scratch.insert(0, pltpu.VMEM((FFN_TOKENS, D_MODEL), F32))
    return pl.pallas_call(
        body,
        out_shape=jax.ShapeDtypeStruct((n_tok, D_MODEL), F32),
        grid=(n_tok // FFN_TOKENS,),
        in_specs=in_specs,
        out_specs=pl.BlockSpec((FFN_TOKENS, D_MODEL), row),
        scratch_shapes=scratch,
        compiler_params=pltpu.CompilerParams(
            dimension_semantics=("arbitrary",), vmem_limit_bytes=VMEM_LIMIT_BYTES),
        name="ffn_ln" if pool is None else "pool_ffn_ln",
    )(*args)


def _interleave_heads(w):
    lead = w.shape[:-1]
    n_tiles = w.shape[-1] // LANES
    w = w.reshape(*lead, n_tiles, 2, HEAD_DIM // LANE_GROUP, LANE_GROUP)
    a = w.ndim - 2
    grp = lambda lo, hi: lax.slice_in_dim(w, lo, hi, axis=a)
    w = jnp.concatenate([grp(0, 1), grp(2, 5), grp(1, 2), grp(5, 8)], axis=a)
    w = jnp.swapaxes(w, a - 1, a)
    return w.reshape(*lead, n_tiles * LANES)


def _rope_expand_matrix():
    e = np.zeros((2 * ROT_HALF, 2 * LANES), np.float32)
    for lane in range(LANES):
        group, r = divmod(lane, LANE_GROUP)
        if group % (LANES // LANE_GROUP // 2) < 2:
            e[r, lane] = 1.0
            e[ROT_HALF + r, LANES + lane] = -1.0 if group < 2 else 1.0
    return np.concatenate([e, e, e], axis=0)


def _band_bias():
    qi = np.arange(BLOCK)[:, None] + BLOCK
    kj = np.arange(2 * BLOCK)[None, :]
    in_win = (qi - kj >= 0) & (qi - kj < WINDOW)
    first = in_win & (kj >= BLOCK)
    return np.where(np.stack([in_win, first]), 0.0, NEG_INF).astype(np.float32)


def _attn_ln_kernel(x_ref, pos_ref, freq_ref, rope_ref, bias_ref, wqkv_ref, bqkv_ref, sink_ref, wo_ref, bo_ref,
                    g_ref, b_ref, o_ref, q_buf, k_buf, v_buf, att_buf, p_buf, e_buf):
    t = pl.program_id(1)

    @pl.when(t == 0)
    def _():
        k_buf[:, :, :BLOCK, :] = jnp.zeros((N_KV_HEADS, 2, BLOCK, LANES), BF16)
        v_buf[:, :, :BLOCK, :] = jnp.zeros((N_KV_HEADS, 2, BLOCK, LANES), BF16)

    qkv = jnp.dot(x_ref[0].astype(BF16), wqkv_ref[...], preferred_element_type=F32) + bqkv_ref[...]

    def qkv_tile(col):
        return qkv[:, col:col + LANES]

    ang = freq_ref[...] * pos_ref[0, 0].astype(F32)
    trig = jnp.concatenate([jnp.cos(ang), jnp.sin(ang)], axis=0)
    hi = trig.astype(BF16)
    rest = trig - hi.astype(F32)
    mid = rest.astype(BF16)
    lo = (rest - mid.astype(F32)).astype(BF16)
    tables = lax.dot_general(jnp.concatenate([hi, mid, lo], axis=0), rope_ref[...], (((0,), (0,)), ((), ())),
                             preferred_element_type=F32)
    lane_group = lax.broadcasted_iota(jnp.int32, (1, LANES), 1) // LANE_GROUP
    lanes_a = lane_group % 2 == 0
    is_rot = lane_group % (LANES // LANE_GROUP // 2) < 2
    cos_t = tables[:, :LANES] + jnp.where(is_rot, 0.0, 1.0)
    sin_signed = tables[:, LANES:]

    def rotary(tile):
        return tile * cos_t + pltpu.roll(tile, LANES // 2, 1) * sin_signed

    scale = HEAD_DIM ** -0.5 * LOG2_E
    for p in range(N_HEAD_TILES):
        q_buf[p] = (rotary(qkv_tile(p * LANES)) * scale).astype(BF16)

    rows = slice(BLOCK, BLOCK + ATTN_TOKENS)
    for kh in range(N_KV_HEADS):
        c, par = kh // 2, kh % 2
        own = lanes_a if par == 0 else jnp.logical_not(lanes_a)
        to_other = LANE_GROUP if par == 0 else LANES - LANE_GROUP
        k_tile = rotary(qkv_tile(D_MODEL + c * LANES))
        v_tile = qkv_tile(D_MODEL + KV_DIM + c * LANES)
        k_own = jnp.where(own, k_tile, 0.0)
        v_moved = pltpu.roll(jnp.where(own, v_tile, 0.0), to_other, 1)
        k_buf[kh, par, rows, :] = k_own.astype(BF16)
        k_buf[kh, 1 - par, rows, :] = pltpu.roll(k_own, to_other, 1).astype(BF16)
        v_buf[kh, par, rows, :] = jnp.where(own, v_tile, 1.0).astype(BF16)
        v_buf[kh, 1 - par, rows, :] = jnp.where(own, 1.0, v_moved).astype(BF16)

    nt = (((1,), (1,)), ((), ()))
    n_blocks = ATTN_TOKENS // BLOCK

    def scores_stage(j, slot):
        r0 = pl.multiple_of(j * BLOCK, BLOCK)
        qrows = pl.ds(r0, BLOCK)
        band = pl.ds(r0, 2 * BLOCK)
        bias = bias_ref[jnp.where(t * n_blocks + j == 0, 1, 0)]
        for kh in range(N_KV_HEADS):
            tiles = [TILES_PER_KV_HEAD * kh + i for i in range(TILES_PER_KV_HEAD)]
            qq = jnp.concatenate([q_buf[p, qrows, :] for p in tiles], axis=0)
            m_all = []
            for par in range(2):
                s = lax.dot_general(qq, k_buf[kh, par, band, :], nt, preferred_element_type=F32)
                ms = []
                for i, p in enumerate(tiles):
                    sink = sink_ref[2 * p + par] * LOG2_E
                    sh = s[i * BLOCK:(i + 1) * BLOCK] + bias
                    m = jnp.maximum(jnp.max(sh, axis=-1, keepdims=True), sink)
                    p_buf[slot, kh, par, i * BLOCK:(i + 1) * BLOCK, :] = jnp.exp2(sh - m).astype(BF16)
                    ms.append(sink - jnp.broadcast_to(m, (BLOCK, LANES)))
                m_all.append(jnp.concatenate(ms, axis=0))
            e_buf[slot, kh] = jnp.exp2(jnp.where(lanes_a, m_all[0], m_all[1]))

    def values_stage(j, slot):
        r0 = pl.multiple_of(j * BLOCK, BLOCK)
        qrows = pl.ds(r0, BLOCK)
        band = pl.ds(r0, 2 * BLOCK)
        for kh in range(N_KV_HEADS):
            o_full = [jnp.dot(p_buf[slot, kh, par], v_buf[kh, par, band, :], preferred_element_type=F32)
                      for par in range(2)]
            num = jnp.where(lanes_a, o_full[0], o_full[1])
            den = pltpu.roll(jnp.where(lanes_a, o_full[1], o_full[0]), LANE_GROUP, 1) + e_buf[slot, kh]
            out = (num / den).astype(BF16)
            for i in range(TILES_PER_KV_HEAD):
                att_buf[TILES_PER_KV_HEAD * kh + i, qrows, :] = out[i * BLOCK:(i + 1) * BLOCK]

    def block_body(j, carry):
        values_stage(j, j % 2)
        scores_stage(j + 1, (j + 1) % 2)
        return carry

    scores_stage(0, 0)
    lax.fori_loop(0, n_blocks - 1, block_body, 0)
    values_stage(n_blocks - 1, (n_blocks - 1) % 2)

    k_buf[:, :, :BLOCK, :] = k_buf[:, :, ATTN_TOKENS:ATTN_TOKENS + BLOCK, :]
    v_buf[:, :, :BLOCK, :] = v_buf[:, :, ATTN_TOKENS:ATTN_TOKENS + BLOCK, :]

    for r in range(ATTN_TOKENS // ATTN_OUT_ROWS):
        rs = slice(r * ATTN_OUT_ROWS, (r + 1) * ATTN_OUT_ROWS)
        att = jnp.concatenate([att_buf[p, rs, :] for p in range(N_HEAD_TILES)], axis=1)
        mix = jnp.dot(att, wo_ref[...], preferred_element_type=F32) + bo_ref[...]
        z = DEEPNORM_ALPHA * x_ref[0, rs, :] + mix
        o_ref[0, rs, :] = _layer_norm(z, g_ref[...], b_ref[...])


def _attn_ln(x, positions, wqkv, bqkv, sinks, wo, bo, g, b):
    bsz, seq, _ = x.shape
    n_tiles = seq // ATTN_TOKENS
    inv_freq = ROPE_THETA ** (-jnp.arange(0, ROT_DIM, 2, dtype=F32) / ROT_DIM)
    split = (D_MODEL, D_MODEL + KV_DIM)
    wqkv_l = jnp.concatenate([_interleave_heads(w) for w in jnp.split(wqkv, split, axis=1)], axis=1)
    bqkv_l = jnp.concatenate([_interleave_heads(w) for w in jnp.split(bqkv, split)])
    wo_l = _interleave_heads(wo.T).T
    return pl.pallas_call(
        _attn_ln_kernel,
        out_shape=jax.ShapeDtypeStruct(x.shape, F32),
        grid=(bsz, n_tiles),
        in_specs=[
            pl.BlockSpec((1, ATTN_TOKENS, D_MODEL), lambda i, t: (i, t, 0)),
            pl.BlockSpec((1, 1, 1, ATTN_TOKENS), lambda i, t: (i, t, 0, 0)),
            _resident((ROT_HALF, 1)),
            _resident((6 * ROT_HALF, 2 * LANES)),
            _resident((2, BLOCK, 2 * BLOCK)),
            _resident((D_MODEL, QKV_DIM)),
            _resident((1, QKV_DIM)),
            pl.BlockSpec(memory_space=pltpu.SMEM),
            _resident((D_MODEL, D_MODEL)),
            _resident((1, D_MODEL)),
            _resident((1, D_MODEL)),
            _resident((1, D_MODEL)),
        ],
        out_specs=pl.BlockSpec((1, ATTN_TOKENS, D_MODEL), lambda i, t: (i, t, 0)),
        scratch_shapes=[
            pltpu.VMEM((N_HEAD_TILES, ATTN_TOKENS, LANES), BF16),
            pltpu.VMEM((N_KV_HEADS, 2, BLOCK + ATTN_TOKENS, LANES), BF16),
            pltpu.VMEM((N_KV_HEADS, 2, BLOCK + ATTN_TOKENS, LANES), BF16),
            pltpu.VMEM((N_HEAD_TILES, ATTN_TOKENS, LANES), BF16),
            pltpu.VMEM((2, N_KV_HEADS, 2, TILES_PER_KV_HEAD * BLOCK, 2 * BLOCK), BF16),
            pltpu.VMEM((2, N_KV_HEADS, TILES_PER_KV_HEAD * BLOCK, LANES), F32),
        ],
        compiler_params=pltpu.CompilerParams(
            dimension_semantics=("arbitrary", "arbitrary"), vmem_limit_bytes=VMEM_LIMIT_BYTES),
        name="attn_ln",
    )(x, positions.reshape(bsz, n_tiles, 1, ATTN_TOKENS), inv_freq.reshape(ROT_HALF, 1),
      jnp.asarray(_rope_expand_matrix(), BF16), jnp.asarray(_band_bias()), wqkv_l.astype(BF16),
      bqkv_l.reshape(1, -1), sinks, wo_l.astype(BF16), bo.reshape(1, -1), g.reshape(1, -1), b.reshape(1, -1))


def kernel(x, positions, ln_g, ln_b, ffn_w_gate, ffn_w_up, ffn_w_down, pool_w, pool_b, pool_scale,
           attn_w_qkv, attn_b_qkv, attn_sinks, attn_w_o, attn_b_o):
    bsz, seq, d = x.shape
    assert seq % FFN_TOKENS == 0 and seq % ATTN_TOKENS == 0 and d == D_MODEL

    def ffn(h, i, s, n, pool=None):
        out = _ffn_ln(h.reshape(bsz * seq, d), ffn_w_gate, ffn_w_up, ffn_w_down, i, s, ln_g[i, n], ln_b[i, n],
                      pool, seq)
        return out.reshape(bsz, seq, d)

    for i in range(DEPTH):
        x = ffn(x, i, 0, 0)
        j = i // 2
        if i % 2 == 0:
            x = ffn(x, i, 1, 2, pool=(pool_w[j], pool_b[j], pool_scale[j], ln_g[i, 1], ln_b[i, 1]))
        else:
            x = _attn_ln(x, positions, attn_w_qkv[j], attn_b_qkv[j], attn_sinks[j], attn_w_o[j], attn_b_o[j],
                         ln_g[i, 1], ln_b[i, 1])
            x = ffn(x, i, 1, 2)
    return x
```
